```python
import functools
import jax, jax.numpy as jnp
from jax import lax
import numpy as np

D_MODEL = 2048
BATCH = 4
SEQ = 2048
DEPTH = 2
DEC_BATCH = 128
DEC_SEQ = 4
PAST_LEN = 16384
PAGE_SIZE = 128

D_MIX = D_MODEL
MLA_V = 128
MLA_HEADS = (D_MIX // 2) // MLA_V
MLA_NOPE = 128
MLA_ROPE = 64
MLA_QK = MLA_NOPE + MLA_ROPE
KV_RANK = 512
ROPE_BASE = 10000.0
ATTN_BLOCK = 128
CONV_CH = D_MIX // 4
CONV_WIDTH = 31
GLA_HEADS = 4
GLA_DV = (D_MIX // 4) // GLA_HEADS
GLA_DK = GLA_DV // 2
GLA_GATE_RANK = 16
GLA_TAU = 16.0
GLA_CHUNK = 16
N_GROUPS = 4
EXPERTS_PER_GROUP = 8
N_EXPERTS = N_GROUPS * EXPERTS_PER_GROUP
TOP_K = 2
D_EXPERT = 704
MOE_BLOCK = 64
EPS = 1e-6
IN_SPLITS = (MLA_HEADS * MLA_QK, KV_RANK, MLA_ROPE, CONV_CH, CONV_CH,
             GLA_HEADS * GLA_DK, GLA_HEADS * GLA_DK, GLA_HEADS * GLA_DV, GLA_GATE_RANK, GLA_HEADS * GLA_DV)
IN_COLS = sum(IN_SPLITS)

kernel_name = 'hymba_mla_conformer_gla_hmoe_step'


def rms_norm(x, g):
    xf = x.astype(jnp.float32)
    y = xf * lax.rsqrt(jnp.mean(xf * xf, axis=-1, keepdims=True) + EPS)
    return (y * g.astype(jnp.float32)).astype(x.dtype)


def layer_norm(x, g, b):
    xf = x.astype(jnp.float32)
    mu = jnp.mean(xf, axis=-1, keepdims=True)
    var = jnp.mean(jnp.square(xf - mu), axis=-1, keepdims=True)
    y = (xf - mu) * lax.rsqrt(var + EPS)
    return (y * g.astype(jnp.float32) + b.astype(jnp.float32)).astype(x.dtype)


def rope(x, pos):
    half = x.shape[-1] // 2
    inv = ROPE_BASE ** (-jnp.arange(half, dtype=jnp.float32) / half)
    ang = pos.astype(jnp.float32)[:, None] * inv[None, :]
    cos = jnp.cos(ang)[:, None, :]
    sin = jnp.sin(ang)[:, None, :]
    xf = x.astype(jnp.float32)
    x1, x2 = xf[..., :half], xf[..., half:]
    return jnp.concatenate([x1 * cos - x2 * sin, x2 * cos + x1 * sin], axis=-1).astype(x.dtype)


def split_cols(z):
    parts, o = [], 0
    for n in IN_SPLITS:
        parts.append(z[..., o:o + n])
        o += n
    return parts


def mla_query(q_raw, pos, q_gain):
    q = q_raw.reshape(q_raw.shape[:-1] + (MLA_HEADS, MLA_QK))
    q = rms_norm(q, q_gain)
    return jnp.concatenate([q[..., :MLA_NOPE], rope(q[..., MLA_NOPE:], pos)], axis=-1)


def mla_keys(c, kr, pos, w_uk, k_gain):
    k_nope = jnp.einsum('...tc,chd->...thd', c, w_uk)
    k_rope = jnp.broadcast_to(kr[..., None, :], k_nope.shape[:-1] + (MLA_ROPE,))
    k = rms_norm(jnp.concatenate([k_nope, k_rope], axis=-1), k_gain)
    return jnp.concatenate([k[..., :MLA_NOPE], rope(k[..., MLA_NOPE:], pos)], axis=-1)


def mla_prompt(q, k, c, w_uv):
    b, s = q.shape[:2]
    nb = s // ATTN_BLOCK
    scale = MLA_QK ** -0.5
    v = jnp.einsum('btc,chv->bthv', c, w_uv)
    qb = q.reshape(b, nb, ATTN_BLOCK, MLA_HEADS, MLA_QK).transpose(1, 0, 2, 3, 4)
    kpos = jnp.arange(s)

    def block(args):
        qi, i = args
        sc = jnp.einsum('bqhd,bkhd->bhqk', qi, k, preferred_element_type=jnp.float32) * scale
        qpos = i * ATTN_BLOCK + jnp.arange(ATTN_BLOCK)
        sc = jnp.where(kpos[None, :] <= qpos[:, None], sc, -jnp.inf)
        p = jax.nn.softmax(sc, axis=-1).astype(v.dtype)
        return jnp.einsum('bhqk,bkhv->bqhv', p, v)

    o = lax.map(block, (qb, jnp.arange(nb)))
    return o.transpose(1, 0, 2, 3, 4).reshape(b, s, MLA_HEADS * MLA_V)


def mla_sample(q, k_new, c_new, cache_lat, cache_kr, page_table, w_uk, w_uv, k_gain):
    t = q.shape[1]
    past = page_table.shape[1] * cache_lat.shape[1]
    pos_past = jnp.arange(past)
    causal = jnp.arange(t)[None, :] <= jnp.arange(t)[:, None]
    scale = MLA_QK ** -0.5

    def one_seq(args):
        qs, kn, cn, pages = args
        c_past = cache_lat[pages].reshape(past, KV_RANK)
        kr_past = cache_kr[pages].reshape(past, MLA_ROPE)
        k_past = mla_keys(c_past, kr_past, pos_past, w_uk, k_gain)
        s_past = jnp.einsum('qhd,khd->hqk', qs, k_past, preferred_element_type=jnp.float32) * scale
        s_new = jnp.einsum('qhd,khd->hqk', qs, kn, preferred_element_type=jnp.float32) * scale
        s_new = jnp.where(causal, s_new, -jnp.inf)
        p = jax.nn.softmax(jnp.concatenate([s_past, s_new], axis=-1), axis=-1).astype(cn.dtype)
        o_lat = jnp.einsum('hqk,kc->qhc', p, jnp.concatenate([c_past, cn], axis=0))
        return jnp.einsum('qhc,chv->qhv', o_lat, w_uv)

    o = lax.map(one_seq, (q, k_new, c_new, page_table))
    return o.reshape(q.shape[0], t, MLA_HEADS * MLA_V)


def conformer_conv(u_ext, conv_w, conv_b, ln_g, ln_b):
    y = lax.conv_general_dilated(u_ext, conv_w[:, None, :].astype(u_ext.dtype), window_strides=(1,),
                                 padding='VALID', dimension_numbers=('NWC', 'WIO', 'NWC'),
                                 feature_group_count=CONV_CH) + conv_b
    return jax.nn.silu(layer_norm(y, ln_g, ln_b))


def gla_chunked(q, k, v, g, s0):
    b, t, h, dk = q.shape
    pad = (-t) % GLA_CHUNK
    padf = lambda a: jnp.pad(a.astype(jnp.float32), ((0, 0), (0, pad), (0, 0), (0, 0)))
    n = (t + pad) // GLA_CHUNK
    rs = lambda a: padf(a).reshape(b, n, GLA_CHUNK, h, a.shape[-1])
    q, k, v, g = rs(q) * dk ** -0.5, rs(k), rs(v), rs(g)
    cum = jnp.cumsum(g, axis=2)
    tri = jnp.tril(jnp.ones((GLA_CHUNK, GLA_CHUNK), bool))[None, None, :, :, None, None]
    decay = jnp.exp(jnp.where(tri, cum[:, :, :, None] - cum[:, :, None, :], -jnp.inf))
    a = jnp.einsum('bnthk,bnshk,bntshk->bnths', q, k, decay)
    o_intra = jnp.einsum('bnths,bnshv->bnthv', a, v)
    last = cum[:, :, -1]
    u = jnp.einsum('bnshk,bnshv->bnhkv', k * jnp.exp(last[:, :, None] - cum), v)
    qd = q * jnp.exp(cum)

    def step(s, xs):
        dec, u_n, q_n = xs
        o = jnp.einsum('bthk,bhkv->bthv', q_n, s)
        return dec[..., None] * s + u_n, o

    s_fin, o_inter = lax.scan(step, s0.astype(jnp.float32),
                              (jnp.exp(last).transpose(1, 0, 2, 3), u.transpose(1, 0, 2, 3, 4),
                               qd.transpose(1, 0, 2, 3, 4)))
    o = o_intra + o_inter.transpose(1, 0, 2, 3, 4)
    return o.reshape(b, n * GLA_CHUNK, h, -1)[:, :t], s_fin.astype(s0.dtype)


def gla_mixer(gq, gk, gv, g_lr, g_out, s0, gate_w, gate_b, norm_g):
    lead = gq.shape[:-1]
    q = gq.reshape(lead + (GLA_HEADS, GLA_DK))
    k = gk.reshape(lead + (GLA_HEADS, GLA_DK))
    v = gv.reshape(lead + (GLA_HEADS, GLA_DV))
    log_a = jax.nn.log_sigmoid((g_lr @ gate_w + gate_b).astype(jnp.float32)) / GLA_TAU
    o, s_new = gla_chunked(q, k, v, log_a.reshape(lead + (GLA_HEADS, GLA_DK)), s0)
    o = rms_norm(o.astype(gq.dtype), norm_g).reshape(lead + (GLA_HEADS * GLA_DV,))
    return o * jax.nn.silu(g_out), s_new


def moe_dispatch(h, experts, gates, w_gate, w_up, w_down):
    n, d = h.shape
    a = n * TOP_K
    e_flat = experts.reshape(a)
    g_flat = gates.reshape(a)
    t_flat = jnp.repeat(jnp.arange(n, dtype=jnp.int32), TOP_K)
    order = jnp.argsort(e_flat)
    e_sorted = e_flat[order]
    counts = jnp.bincount(e_flat, length=N_EXPERTS)
    padded = (counts + MOE_BLOCK - 1) // MOE_BLOCK * MOE_BLOCK
    start = jnp.cumsum(counts) - counts
    pad_end = jnp.cumsum(padded)
    pad_start = pad_end - padded
    dest = pad_start[e_sorted] + jnp.arange(a) - start[e_sorted]
    n_blocks = -(-(a + N_EXPERTS * (MOE_BLOCK - 1)) // MOE_BLOCK)
    rows = n_blocks * MOE_BLOCK
    row_tok = jnp.full((rows,), n, jnp.int32).at[dest].set(t_flat[order])
    row_gate = jnp.zeros((rows,), h.dtype).at[dest].set(g_flat[order])
    block_exp = jnp.minimum(jnp.searchsorted(pad_end, jnp.arange(n_blocks) * MOE_BLOCK, side='right'),
                            N_EXPERTS - 1)
    h_pad = jnp.concatenate([h, jnp.zeros((1, d), h.dtype)], axis=0)

    def run_block(args):
        tok, e = args
        xb = h_pad[tok]
        return (jax.nn.silu(xb @ w_gate[e]) * (xb @ w_up[e])) @ w_down[e]

    out = lax.map(run_block, (row_tok.reshape(n_blocks, MOE_BLOCK), block_exp))
    y = jnp.zeros((n + 1, d), h.dtype).at[row_tok].add(out.reshape(rows, d) * row_gate[:, None])
    return y[:n]


def hier_moe(h, rg_w, rg_b, re_w, re_b, w_gate, w_up, w_down):
    hf = h.reshape(-1, h.shape[-1])
    p_grp = jax.nn.softmax((hf @ rg_w + rg_b).astype(jnp.float32), axis=-1)
    g_prob, g_idx = lax.top_k(p_grp, 1)
    logit_e = (hf @ re_w + re_b).astype(jnp.float32).reshape(-1, N_GROUPS, EXPERTS_PER_GROUP)
    logit_in = jnp.einsum('ng,nge->ne', jax.nn.one_hot(g_idx[:, 0], N_GROUPS, dtype=jnp.float32), logit_e)
    e_prob, e_idx = lax.top_k(jax.nn.softmax(logit_in, axis=-1), TOP_K)
    gates = g_prob * e_prob / jnp.sum(e_prob, axis=-1, keepdims=True)
    experts = g_idx * EXPERTS_PER_GROUP + e_idx
    y = moe_dispatch(hf, experts, gates.astype(h.dtype), w_gate, w_up, w_down)
    return y.reshape(h.shape)


def decoder_layer(x, pos, conv_prev, gla_prev, attend, p):
    (ln_mix, w_in, latent_gain, q_gain, k_gain, w_uk, conv_w, conv_b, conv_ln_g, conv_ln_b,
     gla_gate_w, gla_gate_b, gla_norm_g, w_out, ln_ffn, rg_w, rg_b, re_w, re_b, w_gate, w_up, w_down) = p
    z = rms_norm(x, ln_mix) @ w_in
    q_raw, c_raw, kr, conv_a, conv_gate, gq, gk, gv, g_lr, g_out = split_cols(z)
    c = rms_norm(c_raw, latent_gain)
    q = mla_query(q_raw, pos, q_gain)
    k = mla_keys(c, kr, pos, w_uk, k_gain)
    mla_o = attend(q, k, c)
    u = conv_a * jax.nn.sigmoid(conv_gate)
    u_ext = jnp.concatenate([conv_prev.astype(u.dtype), u], axis=1)
    conv_o = conformer_conv(u_ext, conv_w, conv_b, conv_ln_g, conv_ln_b)
    gla_o, gla_new = gla_mixer(gq, gk, gv, g_lr, g_out, gla_prev, gla_gate_w, gla_gate_b, gla_norm_g)
    x = x + jnp.concatenate([mla_o, conv_o, gla_o], axis=-1) @ w_out
    x = x + hier_moe(rms_norm(x, ln_ffn), rg_w, rg_b, re_w, re_b, w_gate, w_up, w_down)
    return x, c, kr, u_ext[:, -(CONV_WIDTH - 1):], gla_new


def setup_inputs(seed: int = 0) -> dict:
    key = jax.random.key(seed)
    ks = list(jax.random.split(key, 32))

    def nrm(shape, scale=1.0):
        return jax.random.normal(ks.pop(), shape, jnp.float32) * scale

    n_pages = PAST_LEN // PAGE_SIZE
    pool_pages = (DEC_BATCH * n_pages * 5) // 4
    page_table = jax.random.permutation(ks.pop(), pool_pages)[:DEC_BATCH * n_pages]
    page_table = page_table.reshape(DEC_BATCH, n_pages).astype(jnp.int32)
    D = D_MODEL
    return {
        'x_prompt': nrm((BATCH, SEQ, D)),
        'x_sample': nrm((DEC_BATCH, DEC_SEQ, D)),
        'cache_latent': nrm((DEPTH, pool_pages, PAGE_SIZE, KV_RANK)),
        'cache_k_rope': nrm((DEPTH, pool_pages, PAGE_SIZE, MLA_ROPE)),
        'state_conv': nrm((DEPTH, DEC_BATCH, CONV_WIDTH - 1, CONV_CH), 0.5),
        'state_gla': nrm((DEPTH, DEC_BATCH, GLA_HEADS, GLA_DK, GLA_DV), 2.0),
        'page_table': page_table,
        'ln_mix': 1.0 + nrm((DEPTH, D), 0.01),
        'w_in': nrm((DEPTH, D, IN_COLS), D ** -0.5),
        'latent_gain': 1.0 + nrm((DEPTH, KV_RANK), 0.01),
        'q_gain': 1.0 + nrm((DEPTH, MLA_QK), 0.01),
        'k_gain': 1.0 + nrm((DEPTH, MLA_QK), 0.01),
        'w_uk': nrm((DEPTH, KV_RANK, MLA_HEADS, MLA_NOPE), KV_RANK ** -0.5),
        'w_uv': nrm((DEPTH, KV_RANK, MLA_HEADS, MLA_V), KV_RANK ** -0.5),
        'conv_w': nrm((DEPTH, CONV_WIDTH, CONV_CH), CONV_WIDTH ** -0.5),
        'conv_b': nrm((DEPTH, CONV_CH), 0.01),
        'conv_ln_g': 1.0 + nrm((DEPTH, CONV_CH), 0.01),
        'conv_ln_b': nrm((DEPTH, CONV_CH), 0.01),
        'gla_gate_w': nrm((DEPTH, GLA_GATE_RANK, GLA_HEADS * GLA_DK), GLA_GATE_RANK ** -0.5),
        'gla_gate_b': nrm((DEPTH, GLA_HEADS * GLA_DK), 0.1),
        'gla_norm_g': 1.0 + nrm((DEPTH, GLA_DV), 0.01),
        'w_out': nrm((DEPTH, D_MIX, D), D_MIX ** -0.5),
        'ln_ffn': 1.0 + nrm((DEPTH, D), 0.01),
        'router_group_w': nrm((DEPTH, D, N_GROUPS), D ** -0.5),
        'router_group_b': nrm((DEPTH, N_GROUPS), 0.01),
        'router_expert_w': nrm((DEPTH, D, N_EXPERTS), D ** -0.5),
        'router_expert_b': nrm((DEPTH, N_EXPERTS), 0.01),
        'w_gate': nrm((DEPTH, N_EXPERTS, D, D_EXPERT), D ** -0.5),
        'w_up': nrm((DEPTH, N_EXPERTS, D, D_EXPERT), D ** -0.5),
        'w_down': nrm((DEPTH, N_EXPERTS, D_EXPERT, D), D_EXPERT ** -0.5),
    }


def reference(x_prompt, x_sample, cache_latent, cache_k_rope, state_conv, state_gla, page_table,
              ln_mix, w_in, latent_gain, q_gain, k_gain, w_uk, w_uv, conv_w, conv_b, conv_ln_g, conv_ln_b,
              gla_gate_w, gla_gate_b, gla_norm_g, w_out, ln_ffn, router_group_w, router_group_b,
              router_expert_w, router_expert_b, w_gate, w_up, w_down):
    bp, sp = x_prompt.shape[:2]
    past = page_table.shape[1] * cache_latent.shape[2]
    pos_p = jnp.arange(sp)
    pos_s = past + jnp.arange(x_sample.shape[1])
    layer_params = (ln_mix, w_in, latent_gain, q_gain, k_gain, w_uk, conv_w, conv_b, conv_ln_g, conv_ln_b,
                    gla_gate_w, gla_gate_b, gla_norm_g, w_out, ln_ffn, router_group_w, router_group_b,
                    router_expert_w, router_expert_b, w_gate, w_up, w_down)
    xp, xs = x_prompt, x_sample
    lat_p, kr_p, cv_p, gl_p = [], [], [], []
    lat_s, kr_s, cv_s, gl_s = [], [], [], []
    for l in range(DEPTH):
        p = [a[l] for a in layer_params]
        attend_p = functools.partial(mla_prompt, w_uv=w_uv[l])
        xp, c, kr, cv, gl = decoder_layer(
            xp, pos_p, jnp.zeros((bp, CONV_WIDTH - 1, CONV_CH), xp.dtype),
            jnp.zeros((bp, GLA_HEADS, GLA_DK, GLA_DV), xp.dtype), attend_p, p)
        lat_p.append(c); kr_p.append(kr); cv_p.append(cv); gl_p.append(gl)
        attend_s = functools.partial(mla_sample, cache_lat=cache_latent[l], cache_kr=cache_k_rope[l],
                                     page_table=page_table, w_uk=w_uk[l], w_uv=w_uv[l], k_gain=k_gain[l])
        xs, c, kr, cv, gl = decoder_layer(xs, pos_s, state_conv[l], state_gla[l], attend_s, p)
        lat_s.append(c); kr_s.append(kr); cv_s.append(cv); gl_s.append(gl)
    return (xp, xs, jnp.stack(lat_p), jnp.stack(kr_p), jnp.stack(cv_p), jnp.stack(gl_p),
            jnp.stack(lat_s), jnp.stack(kr_s), jnp.stack(cv_s), jnp.stack(gl_s))
```

```python
import functools
import math

import numpy as np
import jax
import jax.numpy as jnp
from jax import lax
from jax.experimental import pallas as pl
from jax.experimental.pallas import tpu as pltpu

F32 = jnp.float32
BF16 = jnp.bfloat16

MLA_HEADS = 8
MLA_NOPE = 128
MLA_ROPE = 64
MLA_V = 128
MLA_QK = MLA_NOPE + MLA_ROPE
KV_RANK = 512
ROPE_BASE = 10000.0
CONV_CH = 512
CONV_WIDTH = 31
GLA_HEADS = 4
GLA_DV = 128
GLA_DK = 64
GLA_GATE_RANK = 16
GLA_TAU = 16.0
N_GROUPS = 4
EXPERTS_PER_GROUP = 8
N_EXPERTS = N_GROUPS * EXPERTS_PER_GROUP
TOP_K = 2
D_EXPERT = 704
EPS = 1e-6
NEG_BIG = -1e30

LANES = 128
SUBLANES = 8
VMEM_LIMIT_BYTES = 56 * 1024 * 1024

Z_QN = 0
Z_QR = 1024
Z_C = 1536
Z_CA = 2048
Z_CG = 2560
Z_GQ = 3072
Z_GK = 3328
Z_GV = 3584
Z_GO = 4096
Z_KRL = 4608
Z_COLS = 4864


def _z_source_columns():
    src = -np.ones((Z_COLS,), np.int64)
    q0 = 0
    for h in range(MLA_HEADS):
        src[Z_QN + h * MLA_NOPE:Z_QN + (h + 1) * MLA_NOPE] = q0 + h * MLA_QK + np.arange(MLA_NOPE)
        src[Z_QR + h * MLA_ROPE:Z_QR + (h + 1) * MLA_ROPE] = q0 + h * MLA_QK + MLA_NOPE + np.arange(MLA_ROPE)
    o = MLA_HEADS * MLA_QK
    src[Z_C:Z_C + KV_RANK] = o + np.arange(KV_RANK); o += KV_RANK
    src[Z_KRL:Z_KRL + MLA_ROPE] = o + np.arange(MLA_ROPE); o += MLA_ROPE
    src[Z_CA:Z_CA + CONV_CH] = o + np.arange(CONV_CH); o += CONV_CH
    src[Z_CG:Z_CG + CONV_CH] = o + np.arange(CONV_CH); o += CONV_CH
    n = GLA_HEADS * GLA_DK
    src[Z_GQ:Z_GQ + n] = o + np.arange(n); o += n
    src[Z_GK:Z_GK + n] = o + np.arange(n); o += n
    n = GLA_HEADS * GLA_DV
    src[Z_GV:Z_GV + n] = o + np.arange(n); o += n
    src[Z_KRL + MLA_ROPE:Z_KRL + MLA_ROPE + GLA_GATE_RANK] = o + np.arange(GLA_GATE_RANK); o += GLA_GATE_RANK
    src[Z_GO:Z_GO + n] = o + np.arange(n); o += n
    return src


def _cparams(sem):
    return pltpu.CompilerParams(dimension_semantics=sem, vmem_limit_bytes=VMEM_LIMIT_BYTES)


def _dot(a, b):
    return jnp.dot(a, b, preferred_element_type=F32)


def _dot_nt(a, b):
    return lax.dot_general(a, b, (((1,), (1,)), ((), ())), preferred_element_type=F32)


def _dot_tn(a, b):
    return lax.dot_general(a, b, (((0,), (0,)), ((), ())), preferred_element_type=F32)


def _split_hi_lo(a):
    hi = a.astype(BF16)
    lo = (a - hi.astype(F32)).astype(BF16)
    return hi, lo


def _rope_tables(pos, period_lanes=LANES):
    half = MLA_ROPE // 2
    inv = ROPE_BASE ** (-np.arange(half, dtype=np.float64) / half)
    lane = np.arange(period_lanes)
    ang = np.asarray(pos, np.float64)[:, None] * inv[lane % half][None, :]
    sign = np.where((lane % MLA_ROPE) < half, -1.0, 1.0)[None, :]
    return np.cos(ang).astype(np.float32), (np.sin(ang) * sign).astype(np.float32)


def _inproj_body(x_ref, g_ref, w_ref, o_ref):
    x = x_ref[...]
    ms = jnp.mean(x * x, axis=-1, keepdims=True)
    xn = (x * lax.rsqrt(ms + EPS)) * g_ref[...]
    o_ref[...] = _dot(xn.astype(BF16), w_ref[...])


def _inproj(x, gain, w, tm):
    n, d = x.shape
    zc = w.shape[1]
    tn = zc // 2
    return pl.pallas_call(
        _inproj_body,
        grid=(2, n // tm),
        in_specs=[pl.BlockSpec((tm, d), lambda j, i: (i, 0)),
                  pl.BlockSpec((1, d), lambda j, i: (0, 0)),
                  pl.BlockSpec((d, tn), lambda j, i: (0, j))],
        out_specs=pl.BlockSpec((tm, tn), lambda j, i: (i, j)),
        out_shape=jax.ShapeDtypeStruct((n, zc), F32),
        compiler_params=_cparams(("arbitrary", "arbitrary")),
        name="inproj",
    )(x, gain, w)


def _swap32(x):
    lane = lax.broadcasted_iota(jnp.int32, x.shape, x.ndim - 1)
    return jnp.where((lane % MLA_ROPE) < MLA_ROPE // 2, pltpu.roll(x, 96, x.ndim - 1), pltpu.roll(x, 32, x.ndim - 1))


def _prep_body(sample, qn_ref, qr_ref, c_ref, krl_ref, cos_ref, sin_ref, lg_ref, qgn_ref, qgr_ref,
               kgn_ref, kgr_ref, wuk_ref, wuv_ref, *outs):
    if sample:
        c_out, qrope_out, qabs_out = outs
    else:
        c_out, q_out, k_out, v_out = outs
    scale = MLA_QK ** -0.5
    cos = cos_ref[...]
    sin = sin_ref[...]
    tm = cos.shape[0]
    lane = lax.broadcasted_iota(jnp.int32, (tm, LANES), 1)
    low_half = lane < MLA_ROPE

    c_raw = c_ref[...]
    c = c_raw * lax.rsqrt(jnp.mean(c_raw * c_raw, axis=-1, keepdims=True) + EPS) * lg_ref[...]
    c_out[...] = c

    qn = qn_ref[...]
    qr = qr_ref[...]
    for hp in range(MLA_HEADS // 2):
        pair = qr[:, hp * LANES:(hp + 1) * LANES]
        pair_sq = pair * pair
        rinv = []
        for h in (2 * hp, 2 * hp + 1):
            qh = qn[:, h * MLA_NOPE:(h + 1) * MLA_NOPE]
            half_sq = jnp.where(low_half if h % 2 == 0 else ~low_half, pair_sq, 0.0)
            ssq = jnp.sum(qh * qh, axis=-1, keepdims=True) + jnp.sum(half_sq, axis=-1, keepdims=True)
            rinv.append(lax.rsqrt(ssq * (1.0 / MLA_QK) + EPS))
        pair_n = pair * jnp.where(low_half, rinv[0], rinv[1]) * qgr_ref[...]
        rot = (pair_n * cos + _swap32(pair_n) * sin) * scale
        for k, h in enumerate((2 * hp, 2 * hp + 1)):
            qh = qn[:, h * MLA_NOPE:(h + 1) * MLA_NOPE] * rinv[k]
            rot_h = jnp.where(low_half if k == 0 else ~low_half, rot, 0.0)
            if sample:
                qa = (qh * (qgn_ref[...] * kgn_ref[...] * scale)).astype(BF16)
                qabs_out[:, h * KV_RANK:(h + 1) * KV_RANK] = _dot_nt(
                    qa, wuk_ref[:, h * MLA_NOPE:(h + 1) * MLA_NOPE]).astype(BF16)
                qrope_out[:, h * LANES:(h + 1) * LANES] = rot_h.astype(BF16)
            else:
                q_out[:, h * 256:h * 256 + LANES] = (qh * (qgn_ref[...] * scale)).astype(BF16)
                q_out[:, h * 256 + LANES:(h + 1) * 256] = rot_h.astype(BF16)

    if not sample:
        cb = c.astype(BF16)
        kn = _dot(cb, wuk_ref[...])
        v_out[...] = _dot(cb, wuv_ref[...]).astype(BF16)
        krl = krl_ref[...]
        kr2 = jnp.where(low_half, krl, pltpu.roll(krl, MLA_ROPE, 1))
        ssq_r = jnp.sum(jnp.where(low_half, krl * krl, 0.0), axis=-1, keepdims=True)
        krg = kr2 * kgr_ref[...]
        krot = krg * cos + pltpu.roll(krg, MLA_ROPE // 2, 1) * sin
        for h in range(MLA_HEADS):
            kh = kn[:, h * MLA_NOPE:(h + 1) * MLA_NOPE]
            ssq = jnp.sum(kh * kh, axis=-1, keepdims=True) + ssq_r
            rinv_k = lax.rsqrt(ssq * (1.0 / MLA_QK) + EPS)
            k_out[:, h * 256:h * 256 + LANES] = (kh * rinv_k * kgn_ref[...]).astype(BF16)
            k_out[:, h * 256 + LANES:(h + 1) * 256] = jnp.where(
                low_half if h % 2 == 0 else ~low_half, krot * rinv_k, 0.0).astype(BF16)


def _prep(z, row0, nrows, tm, cos, sin, sample, lg, qgn, qgr2, kgn, kgr2, wuk, wuv):
    r0 = row0 // tm
    ptiles = cos.shape[0] // tm
    zspec = lambda width, col: pl.BlockSpec((tm, width), lambda i: (r0 + i, col // width))
    const = lambda a: pl.BlockSpec(a.shape, lambda i: (0,) * a.ndim)
    tab = pl.BlockSpec((tm, LANES), lambda i: (i % ptiles, 0))
    in_specs = [zspec(1024, Z_QN), zspec(512, Z_QR), zspec(512, Z_C), zspec(LANES, Z_KRL), tab, tab,
                const(lg), const(qgn), const(qgr2), const(kgn), const(kgr2), const(wuk), const(wuv)]
    row = lambda width: pl.BlockSpec((tm, width), lambda i: (i, 0))
    if sample:
        out_shape = (jax.ShapeDtypeStruct((nrows, KV_RANK), F32),
                     jax.ShapeDtypeStruct((nrows, MLA_HEADS * LANES), BF16),
                     jax.ShapeDtypeStruct((nrows, MLA_HEADS * KV_RANK), BF16))
        out_specs = (row(KV_RANK), row(MLA_HEADS * LANES), row(MLA_HEADS * KV_RANK))
    else:
        out_shape = (jax.ShapeDtypeStruct((nrows, KV_RANK), F32),
                     jax.ShapeDtypeStruct((nrows, MLA_HEADS * 256), BF16),
                     jax.ShapeDtypeStruct((nrows, MLA_HEADS * 256), BF16),
                     jax.ShapeDtypeStruct((nrows, MLA_HEADS * MLA_V), BF16))
        out_specs = (row(KV_RANK), row(MLA_HEADS * 256), row(MLA_HEADS * 256), row(MLA_HEADS * MLA_V))
    return pl.pallas_call(
        functools.partial(_prep_body, sample),
        grid=(nrows // tm,),
        in_specs=in_specs, out_specs=out_specs, out_shape=out_shape,
        compiler_params=_cparams(("arbitrary",)),
        name="mla_prep_sample" if sample else "mla_prep_prompt",
    )(z, z, z, z, cos, sin, lg, qgn, qgr2, kgn, kgr2, wuk, wuv)


def _pattn_body(q_ref, k_ref, v_ref, o_ref, *, tq):
    qi = pl.program_id(2)
    q = q_ref[...]
    row = lax.broadcasted_iota(jnp.int32, (tq, tq), 0)
    col = lax.broadcasted_iota(jnp.int32, (tq, tq), 1)

    def body(j, carry):
        m, l, acc = carry
        k = k_ref[pl.ds(pl.multiple_of(j * tq, tq), tq), :]
        v = v_ref[pl.ds(pl.multiple_of(j * tq, tq), tq), :]
        s = _dot_nt(q, k)
        s = jnp.where((col + j * tq) <= (row + qi * tq), s, NEG_BIG)
        m_new = jnp.maximum(m, jnp.max(s, axis=-1, keepdims=True))
        alpha = jnp.exp(m - m_new)
        p = jnp.exp(s - m_new)
        l = l * alpha + jnp.sum(p, axis=-1, keepdims=True)
        acc = acc * alpha + _dot(p.astype(BF16), v)
        return m_new, l, acc

    m0 = jnp.full((tq, 1), NEG_BIG, F32)
    l0 = jnp.zeros((tq, 1), F32)
    a0 = jnp.zeros((tq, MLA_V), F32)
    m, l, acc = lax.fori_loop(0, qi + 1, body, (m0, l0, a0))
    o_ref[...] = acc / l


def _prompt_attention(q, k, v, bp, sp, tq):
    nq = sp // tq
    return pl.pallas_call(
        functools.partial(_pattn_body, tq=tq),
        grid=(bp, MLA_HEADS, nq),
        in_specs=[pl.BlockSpec((tq, 256), lambda b, h, i: (b * nq + i, h)),
                  pl.BlockSpec((sp, 256), lambda b, h, i: (b, h)),
                  pl.BlockSpec((sp, MLA_V), lambda b, h, i: (b, h))],
        out_specs=pl.BlockSpec((tq, MLA_V), lambda b, h, i: (b * nq + i, h)),
        out_shape=jax.ShapeDtypeStruct((bp * sp, MLA_HEADS * MLA_V), F32),
        compiler_params=_cparams(("arbitrary", "arbitrary", "arbitrary")),
        name="prompt_attention",
    )(q, k, v)


def _sattn_body(pt_ref, lat_hbm, kr_hbm, cnew_ref, krnew_ref, cosb_ref, sinb_ref, loc_ref, cosn_ref, sinn_ref,
                wuk_ref, qabs_ref, qrope_ref, kgr_ref, o_ref, wext, m_ref, l_ref, acc_ref, latbuf, krbuf,
                lsem, ksem, *, layer, pps, sub, nch, ts, n_pages, page):
    b = pl.program_id(0)
    j = pl.program_id(1)
    nbatch = pl.num_programs(0)
    nrow_q = qabs_ref.shape[0]
    nw = MLA_HEADS * MLA_NOPE

    def page_copies(bb, jj, slot):
        out = []
        for i in range(pps):
            pg = pt_ref[bb * n_pages + jj * pps + i]
            out.append(pltpu.make_async_copy(lat_hbm.at[layer, pg], latbuf.at[slot, pl.ds(i * page, page)],
                                             lsem.at[slot]))
            out.append(pltpu.make_async_copy(kr_hbm.at[layer, pg], krbuf.at[slot, pl.ds(i * page, page)],
                                             ksem.at[slot]))
        return out

    @pl.when((b == 0) & (j == 0))
    def _():
        wext[0:nw, :] = wuk_ref[...]
        for cp in page_copies(0, 0, 0):
            cp.start()

    @pl.when(j == 0)
    def _():
        wext[nw:nw + nrow_q, :] = qabs_ref[...]
        m_ref[...] = jnp.full(m_ref.shape, NEG_BIG, F32)
        l_ref[...] = jnp.zeros(l_ref.shape, F32)
        acc_ref[...] = jnp.zeros(acc_ref.shape, F32)

    ones8 = jnp.ones((SUBLANES, MLA_ROPE), BF16)

    def process(c, kr, cos, sin, mask):
        r = c.shape[0]
        cb = c.astype(BF16)
        big = _dot_nt(wext[...], cb)
        kn = big[0:nw, :]
        ssq_n = jnp.sum((kn * kn).reshape(MLA_HEADS, MLA_NOPE, r), axis=1)
        s_lat = big[nw:nw + nrow_q, :]
        kr2 = jnp.concatenate([kr, kr], axis=1)
        krg = kr2 * kgr_ref[...]
        krot = krg * cos + pltpu.roll(krg, MLA_ROPE // 2, 1) * sin
        s_rope = _dot_nt(qrope_ref[...], krot.astype(BF16))
        sq_hi, sq_lo = _split_hi_lo(kr * kr)
        ssq_r = _dot_nt(ones8, sq_hi) + _dot_nt(ones8, sq_lo)
        rinv = lax.rsqrt((ssq_n + ssq_r) * (1.0 / MLA_QK) + EPS)
        s = (s_lat + s_rope) * jnp.concatenate([rinv] * ts, axis=0)
        if mask is not None:
            s = jnp.where(mask, s, NEG_BIG)
        m_old = m_ref[...]
        m_new = jnp.maximum(m_old, jnp.max(s, axis=-1, keepdims=True))
        alpha = jnp.exp(m_old - m_new)
        p = jnp.exp(s - m_new)
        l_ref[...] = l_ref[...] * alpha + jnp.sum(p, axis=-1, keepdims=True)
        acc_ref[...] = acc_ref[...] * alpha + _dot(p.astype(BF16), cb)
        m_ref[...] = m_new

    @pl.when(j < nch)
    def _():
        slot = (b * nch + j) % 2
        last_of_seq = j + 1 == nch
        nb = jnp.where(last_of_seq, b + 1, b)
        nj = jnp.where(last_of_seq, 0, j + 1)

        @pl.when(nb < nbatch)
        def _():
            for cp in page_copies(nb, nj, 1 - slot):
                cp.start()

        for cp in page_copies(b, j, slot):
            cp.wait()
        rsub = sub * page
        for g in range(pps // sub):
            c = latbuf[slot, pl.ds(g * rsub, rsub), :]
            kr = krbuf[slot, pl.ds(g * rsub, rsub), :]
            base = j * (pps // sub) + g
            cb_ = cosb_ref[pl.ds(base, 1), :]
            sb_ = sinb_ref[pl.ds(base, 1), :]
            cos = cb_ * loc_ref[0] - sb_ * loc_ref[1]
            sin = sb_ * loc_ref[2] + cb_ * loc_ref[3]
            process(c, kr, cos, sin, None)

    @pl.when(j == nch)
    def _():
        r = cnew_ref.shape[0]
        key = lax.broadcasted_iota(jnp.int32, (nrow_q, r), 1)
        tok = lax.broadcasted_iota(jnp.int32, (nrow_q, r), 0) // MLA_HEADS
        process(cnew_ref[...], krnew_ref[...], cosn_ref[...], sinn_ref[...], (key < ts) & (key <= tok))
        o_ref[...] = acc_ref[...] / l_ref[...]


def _sample_attention(layer, cache_latent, cache_k_rope, page_table, cnew, krnew, tabs, wuk_t, qabs, qrope, kgr2,
                      pps, sub):
    bs, n_pages = page_table.shape
    page = cache_latent.shape[2]
    nch = n_pages // pps
    ts = qabs.shape[1] // MLA_HEADS
    cosb, sinb, loc, cosn, sinn = tabs
    pt_flat = page_table.reshape(-1)
    const = lambda a: pl.BlockSpec(a.shape, lambda b, j, pt: (0,) * a.ndim)
    per_seq = lambda a: pl.BlockSpec((None,) + a.shape[1:], lambda b, j, pt: (b,) + (0,) * (a.ndim - 1))
    hbm = pl.BlockSpec(memory_space=pl.ANY)
    in_specs = [hbm, hbm, per_seq(cnew), per_seq(krnew), const(cosb), const(sinb), const(loc),
                const(cosn), const(sinn), const(wuk_t), per_seq(qabs), per_seq(qrope), const(kgr2)]
    nrow_q = qabs.shape[1]
    grid_spec = pltpu.PrefetchScalarGridSpec(
        num_scalar_prefetch=1,
        grid=(bs, nch + 1),
        in_specs=in_specs,
        out_specs=pl.BlockSpec((None, nrow_q, KV_RANK), lambda b, j, pt: (b, 0, 0)),
        scratch_shapes=[pltpu.VMEM((MLA_HEADS * MLA_NOPE + nrow_q, KV_RANK), BF16),
                        pltpu.VMEM((nrow_q, 1), F32), pltpu.VMEM((nrow_q, 1), F32),
                        pltpu.VMEM((nrow_q, KV_RANK), F32),
                        pltpu.VMEM((2, pps * page, KV_RANK), F32),
                        pltpu.VMEM((2, pps * page, MLA_ROPE), F32),
                        pltpu.SemaphoreType.DMA((2,)), pltpu.SemaphoreType.DMA((2,))])
    return pl.pallas_call(
        functools.partial(_sattn_body, layer=layer, pps=pps, sub=sub, nch=nch, ts=ts, n_pages=n_pages, page=page),
        grid_spec=grid_spec,
        out_shape=jax.ShapeDtypeStruct((bs, nrow_q, KV_RANK), F32),
        compiler_params=_cparams(("arbitrary", "arbitrary")),
        name="sample_attention",
    )(pt_flat, cache_latent, cache_k_rope, cnew, krnew, cosb, sinb, loc, cosn, sinn, wuk_t, qabs, qrope, kgr2)


def _uvproj_body(o_ref, w_ref, out_ref):
    out_ref[...] = _dot(o_ref[...].astype(BF16), w_ref[...])


def _uv_project(olat, wuv):
    ns = olat.shape[0]
    return pl.pallas_call(
        _uvproj_body,
        grid=(MLA_HEADS,),
        in_specs=[pl.BlockSpec((ns, KV_RANK), lambda h: (0, h)),
                  pl.BlockSpec((KV_RANK, MLA_V), lambda h: (0, h))],
        out_specs=pl.BlockSpec((ns, MLA_V), lambda h: (0, h)),
        out_shape=jax.ShapeDtypeStruct((ns, MLA_HEADS * MLA_V), F32),
        compiler_params=_cparams(("arbitrary",)),
        name="uv_project",
    )(olat, wuv)


CONV_PAD = 32


def _conv_body(a_ref, g_ref, st_ref, w_ref, b_ref, lng_ref, lnb_ref, o_ref, st_out_ref, ext, *, nseq, t, rc):
    ctx = CONV_WIDTH - 1
    u = a_ref[...] * jax.nn.sigmoid(g_ref[...])
    for s in range(nseq):
        ext[s, 0:SUBLANES, :] = jnp.zeros((SUBLANES, CONV_CH), F32)
        ext[s, CONV_PAD - ctx:CONV_PAD, :] = st_ref[s]
        ext[s, CONV_PAD:CONV_PAD + t, :] = u[s * t:(s + 1) * t, :]
        ext[s, CONV_PAD + t:, :] = jnp.zeros((ext.shape[1] - CONV_PAD - t, CONV_CH), F32)
    bias = b_ref[...]
    lng = lng_ref[...]
    lnb = lnb_ref[...]

    def chunk(s, r0):
        span = ((rc + SUBLANES - 1) // SUBLANES) * SUBLANES
        win = ext[s, pl.ds(r0, span + CONV_PAD + SUBLANES), :]
        acc = jnp.zeros((rc, CONV_CH), F32) + bias
        for b in range(SUBLANES):
            off = CONV_PAD - ctx + b
            sh = win[off:off + span + CONV_PAD - SUBLANES, :]
            for a in range((CONV_WIDTH - b + SUBLANES - 1) // SUBLANES):
                w = SUBLANES * a + b
                acc = acc + sh[SUBLANES * a:SUBLANES * a + rc, :] * w_ref[w:w + 1, :]
        mu = jnp.mean(acc, axis=-1, keepdims=True)
        cen = acc - mu
        var = jnp.mean(cen * cen, axis=-1, keepdims=True)
        y = cen * lax.rsqrt(var + EPS) * lng + lnb
        o_ref[pl.ds(s * t + r0, rc), :] = y * jax.nn.sigmoid(y)

    for s in range(nseq):
        if t // rc == 1:
            chunk(s, 0)
        else:
            def body(i, carry, s=s):
                chunk(s, pl.multiple_of(i * rc, rc))
                return carry
            lax.fori_loop(0, t // rc, body, 0)
        st_out_ref[s] = ext[s, t + CONV_PAD - ctx:t + CONV_PAD, :]


def _conv(z, row0, state, nseq, t, cw, cb, lng, lnb):
    bsz = state.shape[0]
    rows = nseq * t
    r0 = row0 // rows
    rc = min(t, 32)
    const = lambda a: pl.BlockSpec(a.shape, lambda i: (0,) * a.ndim)
    return pl.pallas_call(
        functools.partial(_conv_body, nseq=nseq, t=t, rc=rc),
        grid=(bsz // nseq,),
        in_specs=[pl.BlockSpec((rows, CONV_CH), lambda i: (r0 + i, Z_CA // CONV_CH)),
                  pl.BlockSpec((rows, CONV_CH), lambda i: (r0 + i, Z_CG // CONV_CH)),
                  pl.BlockSpec((nseq, CONV_WIDTH - 1, CONV_CH), lambda i: (i, 0, 0)),
                  const(cw), const(cb), const(lng), const(lnb)],
        out_specs=(pl.BlockSpec((rows, CONV_CH), lambda i: (i, 0)),
                   pl.BlockSpec((nseq, CONV_WIDTH - 1, CONV_CH), lambda i: (i, 0, 0))),
        out_shape=(jax.ShapeDtypeStruct((bsz * t, CONV_CH), F32),
                   jax.ShapeDtypeStruct((bsz, CONV_WIDTH - 1, CONV_CH), F32)),
        scratch_shapes=[pltpu.VMEM((nseq, CONV_PAD + max(t, SUBLANES) + SUBLANES, CONV_CH), F32)],
        compiler_params=_cparams(("arbitrary",)),
        name="conv_module_t%d" % t,
    )(z, z, state, cw, cb, lng, lnb)


def _gla_matrices(rows, seq_len):
    t = np.arange(rows)[:, None]
    u = np.arange(rows)[None, :]
    same = (t // seq_len) == (u // seq_len)
    blocks = [same & (u <= t), same & (u > t)]
    nlev = int(round(math.log2(seq_len)))
    for j in range(nlev):
        h = 2 ** j
        mid = (t // (2 * h)) * (2 * h) + h - 1
        second = (t % (2 * h)) >= h
        blocks.append(np.where(second, (u > mid) & (u <= t), (u > t) & (u <= mid)))
    return np.concatenate(blocks, axis=0).astype(np.float32), nlev


def _gla_body(gq_ref, gk_ref, gv_ref, krl_ref, go_ref, s0_ref, gwp_ref, gb_ref, ng_ref, mall_ref,
              o_ref, sout_ref, st, qd_s, kd_s, ghi_s, glo_s, oi_s, *, seq_len, nlev):
    ci = pl.program_id(1)
    rows = gq_ref.shape[0]
    nh, dk, dv = GLA_HEADS, GLA_DK, GLA_DV
    rg = max(seq_len, SUBLANES)
    spg = rg // seq_len

    @pl.when(ci == 0)
    def _():
        st[...] = s0_ref[...]

    a_hi, a_lo = _split_hi_lo(krl_ref[...])
    w_hi, w_lo = _split_hi_lo(gwp_ref[...])
    x = _dot(a_hi, w_hi) + _dot(a_lo, w_hi) + _dot(a_hi, w_lo) + gb_ref[...]
    g = jax.nn.log_sigmoid(x) * (1.0 / GLA_TAU)
    g_hi, g_lo = _split_hi_lo(g)
    mall = mall_ref[...]
    d = _dot(mall, g_hi) + _dot(mall, g_lo)
    q = gq_ref[...] * (dk ** -0.5)
    k = gk_ref[...]
    v = gv_ref[...].astype(BF16)
    lane_head = lax.broadcasted_iota(jnp.int32, (rows, nh * dk), 1) // dk
    tcol = lax.broadcasted_iota(jnp.int32, (rows, 1), 0)
    rowi = lax.broadcasted_iota(jnp.int32, (rows, rows), 0)
    coli = lax.broadcasted_iota(jnp.int32, (rows, rows), 1)

    def head_stack(a):
        return jnp.concatenate([jnp.where(lane_head == h, a, 0.0) for h in range(nh)], axis=0).astype(BF16)

    p = _dot_nt(head_stack(q), k.astype(BF16))
    amat = [jnp.where(rowi == coli, p[h * rows:(h + 1) * rows], 0.0) for h in range(nh)]
    for j in range(nlev):
        half = 2 ** j
        e = jnp.exp(d[(2 + j) * rows:(3 + j) * rows])
        second = (tcol % (2 * half)) >= half
        qj = jnp.where(second, q * e, 0.0)
        kj = jnp.where(second, 0.0, k * e)
        p = _dot_nt(head_stack(qj), kj.astype(BF16))
        same = (rowi // (2 * half)) == (coli // (2 * half))
        amat = [amat[h] + jnp.where(same, p[h * rows:(h + 1) * rows], 0.0) for h in range(nh)]
    for h in range(nh):
        oi_s[:, h * dv:(h + 1) * dv] = _dot(amat[h].astype(BF16), v[:, h * dv:(h + 1) * dv])

    qd_s[...] = q * jnp.exp(d[0:rows])
    kd_s[...] = k * jnp.exp(d[rows:2 * rows])
    ghi_s[...] = g_hi.astype(F32)
    glo_s[...] = g_lo.astype(F32)

    ones = jnp.ones((rg, dv), BF16)
    gl_head = lax.broadcasted_iota(jnp.int32, (rg, nh * dk), 1) // dk
    g_row = lax.broadcasted_iota(jnp.int32, (rg, 1), 0) // seq_len
    st_head = lax.broadcasted_iota(jnp.int32, (nh * dk, dv), 0) // dk

    def group(gi, carry):
        r0 = pl.multiple_of(gi * rg, rg)
        qd = qd_s[pl.ds(r0, rg), :]
        kd = kd_s[pl.ds(r0, rg), :]
        ghi = ghi_s[pl.ds(r0, rg), :].astype(BF16)
        glo = glo_s[pl.ds(r0, rg), :].astype(BF16)
        vg = gv_ref[pl.ds(r0, rg), :].astype(BF16)
        o_inter = [jnp.zeros((rg, dv), F32) for _ in range(nh)]
        for w in range(spg):
            sidx = gi * spg + w
            s_old = st[sidx]
            s_bf = s_old.astype(BF16)
            mine = g_row == w
            qw = jnp.where(mine, qd, 0.0)
            kw = jnp.where(mine, kd, 0.0).astype(BF16)
            bl = (_dot_tn(jnp.where(mine, ghi, jnp.zeros_like(ghi)), ones)
                  + _dot_tn(jnp.where(mine, glo, jnp.zeros_like(glo)), ones))
            upd = jnp.zeros((nh * dk, dv), F32)
            for h in range(nh):
                o_inter[h] = o_inter[h] + _dot(jnp.where(gl_head == h, qw, 0.0).astype(BF16), s_bf)
                u_h = _dot_tn(kw, vg[:, h * dv:(h + 1) * dv])
                upd = upd + jnp.where(st_head == h, u_h, 0.0)
            st[sidx] = jnp.exp(bl) * s_old + upd
        for h in range(nh):
            oi_s[pl.ds(r0, rg), h * dv:(h + 1) * dv] = oi_s[pl.ds(r0, rg), h * dv:(h + 1) * dv] + o_inter[h]
        return carry

    ngroups = rows // rg
    if ngroups == 1:
        group(0, 0)
    else:
        lax.fori_loop(0, ngroups, group, 0)

    go = go_ref[...]
    for h in range(nh):
        oh = oi_s[:, h * dv:(h + 1) * dv]
        on = oh * lax.rsqrt(jnp.mean(oh * oh, axis=-1, keepdims=True) + EPS) * ng_ref[...]
        gh = go[:, h * dv:(h + 1) * dv]
        o_ref[:, h * dv:(h + 1) * dv] = on * (gh * jax.nn.sigmoid(gh))

    @pl.when(ci == pl.num_programs(1) - 1)
    def _():
        sout_ref[...] = st[...]


def _gla(z, row0, s0, rows, seq_len, nchunks, gwp, gb, ng):
    bsz = s0.shape[0]
    nseq = rows // seq_len if nchunks == 1 else 1
    ngrid = bsz // nseq
    mall_np, nlev = _gla_matrices(rows, seq_len)
    mall = jnp.asarray(mall_np, BF16)
    r0 = row0 // rows
    nk = GLA_HEADS * GLA_DK
    nv = GLA_HEADS * GLA_DV
    zspec = lambda width, col: pl.BlockSpec((rows, width), lambda b, c: (r0 + b * nchunks + c, col // width))
    const = lambda a: pl.BlockSpec(a.shape, lambda b, c: (0,) * a.ndim)
    sspec = pl.BlockSpec((nseq, nk, GLA_DV), lambda b, c: (b, 0, 0))
    return pl.pallas_call(
        functools.partial(_gla_body, seq_len=seq_len, nlev=nlev),
        grid=(ngrid, nchunks),
        in_specs=[zspec(nk, Z_GQ), zspec(nk, Z_GK), zspec(nv, Z_GV), zspec(LANES, Z_KRL), zspec(nv, Z_GO),
                  sspec, const(gwp), const(gb), const(ng), const(mall)],
        out_specs=(pl.BlockSpec((rows, nv), lambda b, c: (b * nchunks + c, 0)), sspec),
        out_shape=(jax.ShapeDtypeStruct((ngrid * nchunks * rows, nv), F32),
                   jax.ShapeDtypeStruct((bsz, nk, GLA_DV), F32)),
        scratch_shapes=[pltpu.VMEM((nseq, nk, GLA_DV), F32),
                        pltpu.VMEM((rows, nk), F32), pltpu.VMEM((rows, nk), F32),
                        pltpu.VMEM((rows, nk), F32), pltpu.VMEM((rows, nk), F32),
                        pltpu.VMEM((rows, nv), F32)],
        compiler_params=_cparams(("arbitrary", "arbitrary")),
        name="gla_l%d" % seq_len,
    )(z, z, z, z, z, s0, gwp, gb, ng, mall)


def _outproj_body(x_ref, m_ref, c_ref, g_ref, w_ref, o_ref):
    mix = jnp.concatenate([m_ref[...], c_ref[...], g_ref[...]], axis=1).astype(BF16)
    o_ref[...] = x_ref[...] + _dot(mix, w_ref[...])


def _outproj(x, mla_o, conv_o, gla_o, w, tm):
    n, d = x.shape
    tn = d // 2
    return pl.pallas_call(
        _outproj_body,
        grid=(2, n // tm),
        in_specs=[pl.BlockSpec((tm, tn), lambda j, i: (i, j)),
                  pl.BlockSpec((tm, mla_o.shape[1]), lambda j, i: (i, 0)),
                  pl.BlockSpec((tm, conv_o.shape[1]), lambda j, i: (i, 0)),
                  pl.BlockSpec((tm, gla_o.shape[1]), lambda j, i: (i, 0)),
                  pl.BlockSpec((w.shape[0], tn), lambda j, i: (0, j))],
        out_specs=pl.BlockSpec((tm, tn), lambda j, i: (i, j)),
        out_shape=jax.ShapeDtypeStruct((n, d), F32),
        compiler_params=_cparams(("arbitrary", "arbitrary")),
        name="outproj",
    )(x, mla_o, conv_o, gla_o, w)


def _router_body(x_ref, g_ref, w_ref, b_ref, h_ref, e_ref, gate_ref):
    x = x_ref[...]
    h = (x * lax.rsqrt(jnp.mean(x * x, axis=-1, keepdims=True) + EPS)) * g_ref[...]
    h_ref[...] = h
    h_hi, h_lo = _split_hi_lo(h)
    w_hi, w_lo = _split_hi_lo(w_ref[...])
    logits = _dot(h_hi, w_hi) + _dot(h_lo, w_hi) + _dot(h_hi, w_lo) + b_ref[...]
    lane = lax.broadcasted_iota(jnp.int32, logits.shape, 1)
    far = jnp.int32(4 * LANES)
    red_max = lambda a: jnp.max(a, axis=-1, keepdims=True)
    red_min = lambda a: jnp.min(a, axis=-1, keepdims=True)
    red_sum = lambda a: jnp.sum(a, axis=-1, keepdims=True)

    is_g = lane < N_GROUPS
    eg = jnp.where(is_g, jnp.exp(logits - red_max(jnp.where(is_g, logits, NEG_BIG))), 0.0)
    pg = eg / red_sum(eg)
    g_prob = red_max(pg)
    g_idx = red_min(jnp.where(is_g & (pg == g_prob), lane, far))
    lo = N_GROUPS + g_idx * EXPERTS_PER_GROUP
    in_g = (lane >= lo) & (lane < lo + EXPERTS_PER_GROUP)
    ee = jnp.where(in_g, jnp.exp(logits - red_max(jnp.where(in_g, logits, NEG_BIG))), 0.0)
    pe = ee / red_sum(ee)
    p1 = red_max(pe)
    i1 = red_min(jnp.where(in_g & (pe == p1), lane, far))
    rest = in_g & (lane != i1)
    p2 = red_max(jnp.where(rest, pe, -1.0))
    i2 = red_min(jnp.where(rest & (pe == p2), lane, far))
    den = p1 + p2
    e_ref[...] = jnp.where(lane == 0, i1 - N_GROUPS, jnp.where(lane == 1, i2 - N_GROUPS, 0))
    gate_ref[...] = jnp.where(lane == 0, g_prob * p1 / den, jnp.where(lane == 1, g_prob * p2 / den, 0.0))


def _router(x, gain, w, b, tm):
    n, d = x.shape
    const = lambda a: pl.BlockSpec(a.shape, lambda i: (0,) * a.ndim)
    return pl.pallas_call(
        _router_body,
        grid=(n // tm,),
        in_specs=[pl.BlockSpec((tm, d), lambda i: (i, 0)), const(gain), const(w), const(b)],
        out_specs=(pl.BlockSpec((tm, d), lambda i: (i, 0)),
                   pl.BlockSpec((tm, LANES), lambda i: (i, 0)),
                   pl.BlockSpec((tm, LANES), lambda i: (i, 0))),
        out_shape=(jax.ShapeDtypeStruct((n, d), F32),
                   jax.ShapeDtypeStruct((n, LANES), jnp.int32),
                   jax.ShapeDtypeStruct((n, LANES), F32)),
        compiler_params=_cparams(("arbitrary",)),
        name="moe_router",
    )(x, gain, w, b)


def _ffn_body(bexp_ref, nval_ref, rtok_ref, rasg_ref, h_hbm, gate_ref, wg_ref, wu_ref, wd_ref, oa_hbm,
              xbuf, obuf, wgb, wub, wdb, gsem, ssem, *, bm):
    i = pl.program_id(0)
    nv = nval_ref[i]

    def gather_copy(r):
        return pltpu.make_async_copy(h_hbm.at[rtok_ref[i * bm + r]], xbuf.at[r], gsem)

    def scatter_copy(r):
        return pltpu.make_async_copy(obuf.at[r], oa_hbm.at[rasg_ref[i * bm + r]], ssem)

    @pl.when(nv > 0)
    def _():
        def gstart(r, c):
            gather_copy(r).start()
            return c
        lax.fori_loop(0, bm, gstart, 0)

        prev = bexp_ref[jnp.maximum(i - 1, 0)]
        @pl.when((i == 0) | (bexp_ref[i] != prev))
        def _():
            wgb[...] = wg_ref[...].astype(BF16)
            wub[...] = wu_ref[...].astype(BF16)
            wdb[...] = wd_ref[...].astype(BF16)

        def gwait(r, c):
            gather_copy(r).wait()
            return c
        lax.fori_loop(0, bm, gwait, 0)

        x = xbuf[...].astype(BF16)
        a = _dot(x, wgb[...])
        u = _dot(x, wub[...])
        mid = (a * jax.nn.sigmoid(a) * u).astype(BF16)
        obuf[...] = _dot(mid, wdb[...]) * gate_ref[...]

        def sstart(r, c):
            scatter_copy(r).start()
            return c
        lax.fori_loop(0, nv, sstart, 0)

        def swait(r, c):
            scatter_copy(r).wait()
            return c
        lax.fori_loop(0, nv, swait, 0)


def _expert_ffn(h, block_exp, block_nvalid, row_tok, row_asg, row_gate, w_gate, w_up, w_down, bm):
    n, d = h.shape
    nb = block_exp.shape[0]
    de = w_gate.shape[2]
    wspec = lambda shp: pl.BlockSpec((None,) + shp, lambda i, be, nv, rt, ra: (be[i], 0, 0))
    grid_spec = pltpu.PrefetchScalarGridSpec(
        num_scalar_prefetch=4,
        grid=(nb,),
        in_specs=[pl.BlockSpec(memory_space=pl.ANY),
                  pl.BlockSpec((bm, 1), lambda i, be, nv, rt, ra: (i, 0)),
                  wspec((d, de)), wspec((d, de)), wspec((de, d))],
        out_specs=pl.BlockSpec(memory_space=pl.ANY),
        scratch_shapes=[pltpu.VMEM((bm, d), F32), pltpu.VMEM((bm, d), F32),
                        pltpu.VMEM((d, de), BF16), pltpu.VMEM((d, de), BF16), pltpu.VMEM((de, d), BF16),
                        pltpu.SemaphoreType.DMA(()), pltpu.SemaphoreType.DMA(())])
    return pl.pallas_call(
        functools.partial(_ffn_body, bm=bm),
        grid_spec=grid_spec,
        out_shape=jax.ShapeDtypeStruct((TOP_K * n, d), F32),
        compiler_params=_cparams(("arbitrary",)),
        name="expert_ffn",
    )(block_exp, block_nvalid, row_tok, row_asg, h, row_gate, w_gate, w_up, w_down)


def _combine_body(x_ref, oa_ref, o_ref):
    d = x_ref.shape[1]
    blk = oa_ref[...]
    o_ref[...] = x_ref[...] + (blk[:, 0:d] + blk[:, d:2 * d])


def _combine(x, oa, tm):
    n, d = x.shape
    return pl.pallas_call(
        _combine_body,
        grid=(n // tm,),
        in_specs=[pl.BlockSpec((tm, d), lambda i: (i, 0)), pl.BlockSpec((tm, TOP_K * d), lambda i: (i, 0))],
        out_specs=pl.BlockSpec((tm, d), lambda i: (i, 0)),
        out_shape=jax.ShapeDtypeStruct((n, d), F32),
        compiler_params=_cparams(("arbitrary",)),
        name="moe_combine",
    )(x, oa.reshape(n, TOP_K * d))


def _dispatch_plan(experts, gates, bm):
    n = experts.shape[0]
    a = n * TOP_K
    e_flat = experts.reshape(a)
    g_flat = gates.reshape(a)
    onehot = (e_flat[:, None] == jnp.arange(N_EXPERTS, dtype=jnp.int32)[None, :]).astype(jnp.int32)
    csum = jnp.cumsum(onehot, axis=0)
    counts = csum[-1]
    rank = jnp.take_along_axis(csum, e_flat[:, None], axis=1)[:, 0] - 1
    padded = (counts + bm - 1) // bm * bm
    pad_end = jnp.cumsum(padded)
    pad_start = pad_end - padded
    dest = pad_start[e_flat] + rank
    nb = -(-(a + N_EXPERTS * (bm - 1)) // bm)
    rows = nb * bm
    row_asg = jnp.full((rows,), -1, jnp.int32).at[dest].set(jnp.arange(a, dtype=jnp.int32))
    valid = row_asg >= 0
    safe = jnp.maximum(row_asg, 0)
    row_tok = jnp.where(valid, safe // TOP_K, 0).astype(jnp.int32)
    row_gate = jnp.where(valid, g_flat[safe], 0.0).astype(F32)[:, None]
    starts = jnp.arange(nb, dtype=jnp.int32) * bm
    block_exp = jnp.minimum(jnp.searchsorted(pad_end, starts, side="right"), N_EXPERTS - 1).astype(jnp.int32)
    block_nvalid = jnp.clip(counts[block_exp] - (starts - pad_start[block_exp]), 0, bm).astype(jnp.int32)
    block_nvalid = jnp.where(starts < pad_end[-1], block_nvalid, 0)
    return block_exp, block_nvalid, row_tok, safe.astype(jnp.int32), row_gate


FFN_BLOCK_ROWS = 256


def kernel(x_prompt, x_sample, cache_latent, cache_k_rope, state_conv, state_gla, page_table, ln_mix, w_in,
           latent_gain, q_gain, k_gain, w_uk, w_uv, conv_w, conv_b, conv_ln_g, conv_ln_b, gla_gate_w, gla_gate_b,
           gla_norm_g, w_out, ln_ffn, router_group_w, router_group_b, router_expert_w, router_expert_b,
           w_gate, w_up, w_down):
    bp, sp, d = x_prompt.shape
    bs, ts, _ = x_sample.shape
    depth = w_in.shape[0]
    n_p, n_s = bp * sp, bs * ts
    n = n_p + n_s
    n_pages = page_table.shape[1]
    page = cache_latent.shape[2]
    past = n_pages * page
    tm = math.gcd(math.gcd(n_p, n_s), 512)
    tmb = math.gcd(tm, 256)
    tq = math.gcd(sp, 256)
    pps = min(16, n_pages)
    sub = min(4, pps)
    assert n_pages % pps == 0 and pps % sub == 0 and sp % tmb == 0 and tmb % ts == 0
    gla_rows = min(128, sp)
    gla_rows_s = min(128, n_s)
    assert sp % gla_rows == 0 and n_s % gla_rows_s == 0 and gla_rows_s % max(ts, SUBLANES) == 0
    assert n_p % gla_rows_s == 0 and (ts & (ts - 1)) == 0
    conv_nseq = SUBLANES
    assert bs % conv_nseq == 0 and n_p % (conv_nseq * ts) == 0

    cos_p, sin_p = _rope_tables(np.arange(sp))
    cos_s, sin_s = _rope_tables(past + (np.arange(tmb) % ts))
    cos_n, sin_n = _rope_tables(past + np.arange(page))
    rsub = sub * page
    half = MLA_ROPE // 2
    inv = ROPE_BASE ** (-np.arange(half, dtype=np.float64) / half)
    lane = np.arange(LANES)
    sign = np.where((lane % MLA_ROPE) < half, -1.0, 1.0)[None, :]
    ang_b = (np.arange(past // rsub, dtype=np.float64) * rsub)[:, None] * inv[lane % half][None, :]
    ang_l = np.arange(rsub, dtype=np.float64)[:, None] * inv[lane % half][None, :]
    cos_b, sin_b = np.cos(ang_b).astype(np.float32), np.sin(ang_b).astype(np.float32)
    loc = np.stack([np.cos(ang_l), np.sin(ang_l), sign * np.cos(ang_l), sign * np.sin(ang_l)]).astype(np.float32)
    tabs = tuple(jnp.asarray(t) for t in (cos_b, sin_b, loc, cos_n, sin_n))
    cos_p, sin_p, cos_s, sin_s = (jnp.asarray(t) for t in (cos_p, sin_p, cos_s, sin_s))

    src = _z_source_columns()
    src_idx = jnp.asarray(np.maximum(src, 0), jnp.int32)
    src_ok = jnp.asarray(src >= 0)

    x = jnp.concatenate([x_prompt.reshape(n_p, d), x_sample.reshape(n_s, d)], axis=0)
    zeros_conv = jnp.zeros((bp, CONV_WIDTH - 1, CONV_CH), F32)
    zeros_gla = jnp.zeros((bp, GLA_HEADS * GLA_DK, GLA_DV), F32)
    outs = [[] for _ in range(8)]

    for l in range(depth):
        row2 = lambda a: a.reshape(1, -1).astype(F32)
        w_in_p = jnp.where(src_ok[None, :], jnp.take(w_in[l], src_idx, axis=1), 0.0).astype(BF16)
        wuk = w_uk[l].reshape(KV_RANK, MLA_HEADS * MLA_NOPE).astype(BF16)
        wuv = w_uv[l].reshape(KV_RANK, MLA_HEADS * MLA_V).astype(BF16)
        qgn, kgn = row2(q_gain[l, :MLA_NOPE]), row2(k_gain[l, :MLA_NOPE])
        qgr2 = row2(jnp.tile(q_gain[l, MLA_NOPE:], 2))
        kgr2 = row2(jnp.tile(k_gain[l, MLA_NOPE:], 2))
        lg = row2(latent_gain[l])

        z = _inproj(x, row2(ln_mix[l]), w_in_p, tm)

        c_p, q_p, k_p, v_p = _prep(z, 0, n_p, tmb, cos_p, sin_p, False, lg, qgn, qgr2, kgn, kgr2, wuk, wuv)
        c_s, qrope_s, qabs_s = _prep(z, n_p, n_s, tmb, cos_s, sin_s, True, lg, qgn, qgr2, kgn, kgr2, wuk, wuv)
        mla_p = _prompt_attention(q_p, k_p, v_p, bp, sp, tq)
        kr_all = z[:, Z_KRL:Z_KRL + MLA_ROPE]
        cnew = jnp.pad(c_s.reshape(bs, ts, KV_RANK), ((0, 0), (0, page - ts), (0, 0)))
        krnew = jnp.pad(kr_all[n_p:].reshape(bs, ts, MLA_ROPE), ((0, 0), (0, page - ts), (0, 0)))
        o_lat = _sample_attention(l, cache_latent, cache_k_rope, page_table, cnew, krnew, tabs, wuk.T,
                                  qabs_s.reshape(bs, ts * MLA_HEADS, KV_RANK),
                                  qrope_s.reshape(bs, ts * MLA_HEADS, LANES), kgr2, pps, sub)
        mla_s = _uv_project(o_lat.reshape(n_s, MLA_HEADS * KV_RANK), wuv)

        cw, cb = conv_w[l].astype(F32), row2(conv_b[l])
        clg, clb = row2(conv_ln_g[l]), row2(conv_ln_b[l])
        conv_p, cst_p = _conv(z, 0, zeros_conv, 1, sp, cw, cb, clg, clb)
        conv_s, cst_s = _conv(z, n_p, state_conv[l], conv_nseq, ts, cw, cb, clg, clb)

        gwp = jnp.zeros((LANES, GLA_HEADS * GLA_DK), F32).at[MLA_ROPE:MLA_ROPE + GLA_GATE_RANK].set(gla_gate_w[l])
        ggb, gng = row2(gla_gate_b[l]), row2(gla_norm_g[l])
        gla_p, gst_p = _gla(z, 0, zeros_gla, gla_rows, gla_rows, sp // gla_rows, gwp, ggb, gng)
        gla_s, gst_s = _gla(z, n_p, state_gla[l].reshape(bs, GLA_HEADS * GLA_DK, GLA_DV), gla_rows_s, ts, 1,
                            gwp, ggb, gng)

        x = _outproj(x, jnp.concatenate([mla_p, mla_s], axis=0), jnp.concatenate([conv_p, conv_s], axis=0),
                     jnp.concatenate([gla_p, gla_s], axis=0), w_out[l].astype(BF16), tm)

        wr = jnp.zeros((d, LANES), F32)
        wr = wr.at[:, :N_GROUPS].set(router_group_w[l]).at[:, N_GROUPS:N_GROUPS + N_EXPERTS].set(router_expert_w[l])
        br = jnp.zeros((1, LANES), F32)
        br = br.at[0, :N_GROUPS].set(router_group_b[l]).at[0, N_GROUPS:N_GROUPS + N_EXPERTS].set(router_expert_b[l])
        h, e_idx, gate = _router(x, row2(ln_ffn[l]), wr, br, tm)
        plan = _dispatch_plan(e_idx[:, :TOP_K], gate[:, :TOP_K], FFN_BLOCK_ROWS)
        oa = _expert_ffn(h, *plan, w_gate[l], w_up[l], w_down[l], FFN_BLOCK_ROWS)
        x = _combine(x, oa, tm)

        outs[0].append(c_p.reshape(bp, sp, KV_RANK))
        outs[1].append(kr_all[:n_p].reshape(bp, sp, MLA_ROPE))
        outs[2].append(cst_p)
        outs[3].append(gst_p.reshape(bp, GLA_HEADS, GLA_DK, GLA_DV))
        outs[4].append(c_s.reshape(bs, ts, KV_RANK))
        outs[5].append(kr_all[n_p:].reshape(bs, ts, MLA_ROPE))
        outs[6].append(cst_s)
        outs[7].append(gst_s.reshape(bs, GLA_HEADS, GLA_DK, GLA_DV))

    return (x[:n_p].reshape(bp, sp, d), x[n_p:].reshape(bs, ts, d)) + tuple(jnp.stack(o) for o in outs)
```

```python
import functools
import math

import numpy as np
import jax
import jax.numpy as jnp
from jax import lax
from jax.experimental import pallas as pl
from jax.experimental.pallas import tpu as pltpu

F32 = jnp.float32
BF16 = jnp.bfloat16

MLA_HEADS = 8
MLA_NOPE = 128
MLA_ROPE = 64
MLA_V = 128
MLA_QK = MLA_NOPE + MLA_ROPE
KV_RANK = 512
ROPE_BASE = 10000.0
CONV_CH = 512
CONV_WIDTH = 31
GLA_HEADS = 4
GLA_DV = 128
GLA_DK = 64
GLA_GATE_RANK = 16
GLA_TAU = 16.0
N_GROUPS = 4
EXPERTS_PER_GROUP = 8
N_EXPERTS = N_GROUPS * EXPERTS_PER_GROUP
TOP_K = 2
D_EXPERT = 704
EPS = 1e-6
NEG_BIG = -1e30

LANES = 128
SUBLANES = 8
VMEM_LIMIT_BYTES = 56 * 1024 * 1024
FFN_VMEM_LIMIT_BYTES = 60 * 1024 * 1024

Z_QN = 0
Z_QR = 1024
Z_C = 1536
Z_CA = 2048
Z_CG = 2560
Z_GQ = 3072
Z_GK = 3328
Z_GV = 3584
Z_GO = 4096
Z_KRL = 4608
Z_COLS = 4864


def _z_source_columns():
    src = -np.ones((Z_COLS,), np.int64)
    q0 = 0
    for h in range(MLA_HEADS):
        src[Z_QN + h * MLA_NOPE:Z_QN + (h + 1) * MLA_NOPE] = q0 + h * MLA_QK + np.arange(MLA_NOPE)
        src[Z_QR + h * MLA_ROPE:Z_QR + (h + 1) * MLA_ROPE] = q0 + h * MLA_QK + MLA_NOPE + np.arange(MLA_ROPE)
    o = MLA_HEADS * MLA_QK
    src[Z_C:Z_C + KV_RANK] = o + np.arange(KV_RANK); o += KV_RANK
    src[Z_KRL:Z_KRL + MLA_ROPE] = o + np.arange(MLA_ROPE); o += MLA_ROPE
    src[Z_CA:Z_CA + CONV_CH] = o + np.arange(CONV_CH); o += CONV_CH
    src[Z_CG:Z_CG + CONV_CH] = o + np.arange(CONV_CH); o += CONV_CH
    n = GLA_HEADS * GLA_DK
    src[Z_GQ:Z_GQ + n] = o + np.arange(n); o += n
    src[Z_GK:Z_GK + n] = o + np.arange(n); o += n
    n = GLA_HEADS * GLA_DV
    src[Z_GV:Z_GV + n] = o + np.arange(n); o += n
    src[Z_KRL + MLA_ROPE:Z_KRL + MLA_ROPE + GLA_GATE_RANK] = o + np.arange(GLA_GATE_RANK); o += GLA_GATE_RANK
    src[Z_GO:Z_GO + n] = o + np.arange(n); o += n
    return src


def _cparams(sem):
    return pltpu.CompilerParams(dimension_semantics=sem, vmem_limit_bytes=VMEM_LIMIT_BYTES)


def _dot(a, b):
    return jnp.dot(a, b, preferred_element_type=F32)


def _dot_nt(a, b):
    return lax.dot_general(a, b, (((1,), (1,)), ((), ())), preferred_element_type=F32)


def _dot_tn(a, b):
    return lax.dot_general(a, b, (((0,), (0,)), ((), ())), preferred_element_type=F32)


def _split_hi_lo(a):
    hi = a.astype(BF16)
    lo = (a - hi.astype(F32)).astype(BF16)
    return hi, lo


def _rope_tables(pos, period_lanes=LANES):
    half = MLA_ROPE // 2
    inv = ROPE_BASE ** (-np.arange(half, dtype=np.float64) / half)
    lane = np.arange(period_lanes)
    ang = np.asarray(pos, np.float64)[:, None] * inv[lane % half][None, :]
    sign = np.where((lane % MLA_ROPE) < half, -1.0, 1.0)[None, :]
    return np.cos(ang).astype(np.float32), (np.sin(ang) * sign).astype(np.float32)


def _inproj_body(x_ref, g_ref, w_ref, o_ref):
    x = x_ref[...]
    ms = jnp.mean(x * x, axis=-1, keepdims=True)
    xn = (x * lax.rsqrt(ms + EPS)) * g_ref[...]
    o_ref[...] = _dot_nt(xn.astype(BF16), w_ref[...])


def _inproj(x, gain, w, tm):
    n, d = x.shape
    zc = w.shape[0]
    tn = zc // 2
    return pl.pallas_call(
        _inproj_body,
        grid=(2, n // tm),
        in_specs=[pl.BlockSpec((tm, d), lambda j, i: (i, 0)),
                  pl.BlockSpec((1, d), lambda j, i: (0, 0)),
                  pl.BlockSpec((tn, d), lambda j, i: (j, 0))],
        out_specs=pl.BlockSpec((tm, tn), lambda j, i: (i, j)),
        out_shape=jax.ShapeDtypeStruct((n, zc), F32),
        compiler_params=_cparams(("arbitrary", "arbitrary")),
        name="inproj",
    )(x, gain, w)


def _swap32(x):
    lane = lax.broadcasted_iota(jnp.int32, x.shape, x.ndim - 1)
    return jnp.where((lane % MLA_ROPE) < MLA_ROPE // 2, pltpu.roll(x, 96, x.ndim - 1), pltpu.roll(x, 32, x.ndim - 1))


def _prep_body(sample, qn_ref, qr_ref, c_ref, krl_ref, cos_ref, sin_ref, lg_ref, qgn_ref, qgr_ref,
               kgn_ref, kgr_ref, wuk_ref, wuv_ref, *outs):
    if sample:
        c_out, qrope_out, qabs_out = outs
    else:
        c_out, q_out, k_out, v_out = outs
    scale = MLA_QK ** -0.5
    cos = cos_ref[...]
    sin = sin_ref[...]
    tm = cos.shape[0]
    lane = lax.broadcasted_iota(jnp.int32, (tm, LANES), 1)
    low_half = lane < MLA_ROPE

    c_raw = c_ref[...]
    c = c_raw * lax.rsqrt(jnp.mean(c_raw * c_raw, axis=-1, keepdims=True) + EPS) * lg_ref[...]
    c_out[...] = c

    qn = qn_ref[...]
    qr = qr_ref[...]
    for hp in range(MLA_HEADS // 2):
        pair = qr[:, hp * LANES:(hp + 1) * LANES]
        pair_sq = pair * pair
        rinv = []
        for h in (2 * hp, 2 * hp + 1):
            qh = qn[:, h * MLA_NOPE:(h + 1) * MLA_NOPE]
            half_sq = jnp.where(low_half if h % 2 == 0 else ~low_half, pair_sq, 0.0)
            ssq = jnp.sum(qh * qh, axis=-1, keepdims=True) + jnp.sum(half_sq, axis=-1, keepdims=True)
            rinv.append(lax.rsqrt(ssq * (1.0 / MLA_QK) + EPS))
        pair_n = pair * jnp.where(low_half, rinv[0], rinv[1]) * qgr_ref[...]
        rot = (pair_n * cos + _swap32(pair_n) * sin) * scale
        if sample:
            qrope_out[:, hp * LANES:(hp + 1) * LANES] = rot.astype(BF16)
        for k, h in enumerate((2 * hp, 2 * hp + 1)):
            qh = qn[:, h * MLA_NOPE:(h + 1) * MLA_NOPE] * rinv[k]
            rot_h = jnp.where(low_half if k == 0 else ~low_half, rot, 0.0)
            if sample:
                qa = (qh * (qgn_ref[...] * kgn_ref[...] * scale)).astype(BF16)
                qabs_out[:, h * KV_RANK:(h + 1) * KV_RANK] = _dot_nt(
                    qa, wuk_ref[:, h * MLA_NOPE:(h + 1) * MLA_NOPE]).astype(BF16)
            else:
                q_out[:, h * 256:h * 256 + LANES] = (qh * (qgn_ref[...] * scale)).astype(BF16)
                q_out[:, h * 256 + LANES:(h + 1) * 256] = rot_h.astype(BF16)

    if not sample:
        cb = c.astype(BF16)
        kn = _dot(cb, wuk_ref[...])
        v_out[...] = _dot(cb, wuv_ref[...]).astype(BF16)
        krl = krl_ref[...]
        kr2 = jnp.where(low_half, krl, pltpu.roll(krl, MLA_ROPE, 1))
        ssq_r = jnp.sum(jnp.where(low_half, krl * krl, 0.0), axis=-1, keepdims=True)
        krg = kr2 * kgr_ref[...]
        krot = krg * cos + pltpu.roll(krg, MLA_ROPE // 2, 1) * sin
        for h in range(MLA_HEADS):
            kh = kn[:, h * MLA_NOPE:(h + 1) * MLA_NOPE]
            ssq = jnp.sum(kh * kh, axis=-1, keepdims=True) + ssq_r
            rinv_k = lax.rsqrt(ssq * (1.0 / MLA_QK) + EPS)
            k_out[:, h * 256:h * 256 + LANES] = (kh * rinv_k * kgn_ref[...]).astype(BF16)
            k_out[:, h * 256 + LANES:(h + 1) * 256] = jnp.where(
                low_half if h % 2 == 0 else ~low_half, krot * rinv_k, 0.0).astype(BF16)


def _prep(z, row0, nrows, tm, cos, sin, sample, lg, qgn, qgr2, kgn, kgr2, wuk, wuv):
    r0 = row0 // tm
    ptiles = cos.shape[0] // tm
    zspec = lambda width, col: pl.BlockSpec((tm, width), lambda i: (r0 + i, col // width))
    const = lambda a: pl.BlockSpec(a.shape, lambda i: (0,) * a.ndim)
    tab = pl.BlockSpec((tm, LANES), lambda i: (i % ptiles, 0))
    in_specs = [zspec(1024, Z_QN), zspec(512, Z_QR), zspec(512, Z_C), zspec(LANES, Z_KRL), tab, tab,
                const(lg), const(qgn), const(qgr2), const(kgn), const(kgr2), const(wuk), const(wuv)]
    row = lambda width: pl.BlockSpec((tm, width), lambda i: (i, 0))
    if sample:
        out_shape = (jax.ShapeDtypeStruct((nrows, KV_RANK), F32),
                     jax.ShapeDtypeStruct((nrows, MLA_HEADS * MLA_ROPE), BF16),
                     jax.ShapeDtypeStruct((nrows, MLA_HEADS * KV_RANK), BF16))
        out_specs = (row(KV_RANK), row(MLA_HEADS * MLA_ROPE), row(MLA_HEADS * KV_RANK))
    else:
        out_shape = (jax.ShapeDtypeStruct((nrows, KV_RANK), F32),
                     jax.ShapeDtypeStruct((nrows, MLA_HEADS * 256), BF16),
                     jax.ShapeDtypeStruct((nrows, MLA_HEADS * 256), BF16),
                     jax.ShapeDtypeStruct((nrows, MLA_HEADS * MLA_V), BF16))
        out_specs = (row(KV_RANK), row(MLA_HEADS * 256), row(MLA_HEADS * 256), row(MLA_HEADS * MLA_V))
    return pl.pallas_call(
        functools.partial(_prep_body, sample),
        grid=(nrows // tm,),
        in_specs=in_specs, out_specs=out_specs, out_shape=out_shape,
        compiler_params=_cparams(("arbitrary",)),
        name="mla_prep_sample" if sample else "mla_prep_prompt",
    )(z, z, z, z, cos, sin, lg, qgn, qgr2, kgn, kgr2, wuk, wuv)


def _pattn_body(q_ref, k_ref, v_ref, o_ref, *, tq):
    sp = q_ref.shape[0]
    row = lax.broadcasted_iota(jnp.int32, (tq, tq), 0)
    col = lax.broadcasted_iota(jnp.int32, (tq, tq), 1)
    for qi in range(sp // tq):
        q = q_ref[qi * tq:(qi + 1) * tq, :]
        s_d = jnp.where(col <= row, _dot_nt(q, k_ref[qi * tq:(qi + 1) * tq, :]), NEG_BIG)
        m = jnp.max(s_d, axis=-1, keepdims=True)
        if qi > 0:
            s_p = _dot_nt(q, k_ref[0:qi * tq, :])
            m = jnp.maximum(m, jnp.max(s_p, axis=-1, keepdims=True))
        p_d = jnp.exp(s_d - m)
        l = jnp.sum(p_d, axis=-1, keepdims=True)
        acc = _dot(p_d.astype(BF16), v_ref[qi * tq:(qi + 1) * tq, :])
        if qi > 0:
            p_p = jnp.exp(s_p - m)
            l = l + jnp.sum(p_p, axis=-1, keepdims=True)
            acc = acc + _dot(p_p.astype(BF16), v_ref[0:qi * tq, :])
        o_ref[qi * tq:(qi + 1) * tq, :] = acc / l


def _prompt_attention(q, k, v, bp, sp, tq):
    return pl.pallas_call(
        functools.partial(_pattn_body, tq=tq),
        grid=(bp, MLA_HEADS),
        in_specs=[pl.BlockSpec((sp, 256), lambda b, h: (b, h)),
                  pl.BlockSpec((sp, 256), lambda b, h: (b, h)),
                  pl.BlockSpec((sp, MLA_V), lambda b, h: (b, h))],
        out_specs=pl.BlockSpec((sp, MLA_V), lambda b, h: (b, h)),
        out_shape=jax.ShapeDtypeStruct((bp * sp, MLA_HEADS * MLA_V), F32),
        compiler_params=_cparams(("arbitrary", "arbitrary")),
        name="prompt_attention",
    )(q, k, v)


SATTN_PPS = 32
SATTN_SUB = 4


def _sample_rope_tables(past, page, rsub):
    half = MLA_ROPE // 2
    inv = (ROPE_BASE ** (-np.arange(half, dtype=np.float64) / half))[:, None]
    ang_b = inv[None] * (np.arange(past // rsub, dtype=np.float64) * rsub)[:, None, None]
    ang_b = np.broadcast_to(ang_b, (past // rsub, half, LANES))
    ang_l = inv * np.arange(rsub, dtype=np.float64)[None, :]
    ang_n = inv * (past + np.arange(page, dtype=np.float64))[None, :]
    f = lambda a: jnp.asarray(a.astype(np.float32))
    return (f(np.cos(ang_b)), f(np.sin(ang_b)), f(np.stack([np.cos(ang_l), np.sin(ang_l)])),
            f(np.cos(ang_n)), f(np.sin(ang_n)))


def _sattn_body(pt_ref, lat_hbm, kr_hbm, cnew_ref, krnew_ref, cosb_ref, sinb_ref, loc_ref, cosn_ref, sinn_ref,
                wuk_ref, qabs_ref, qrope_ref, kgr_ref, o_ref, wext, m_ref, l_ref, acc_ref, latbuf, krbuf,
                lsem, ksem, *, layer, pps, sub, nch, ts, n_pages, page):
    b = pl.program_id(0)
    j = pl.program_id(1)
    nbatch = pl.num_programs(0)
    nrow_q = qabs_ref.shape[0]
    nw = MLA_HEADS * MLA_NOPE
    half_r = MLA_ROPE // 2

    def page_copies(bb, jj, slot):
        out = []
        for i in range(pps):
            pg = pt_ref[bb * n_pages + jj * pps + i]
            out.append(pltpu.make_async_copy(lat_hbm.at[layer, pg], latbuf.at[slot, pl.ds(i * page, page)],
                                             lsem.at[slot]))
            out.append(pltpu.make_async_copy(kr_hbm.at[layer, pg], krbuf.at[slot, :, pl.ds(i * page, page)],
                                             ksem.at[slot]))
        return out

    @pl.when((b == 0) & (j == 0))
    def _():
        wext[0:nw, :] = wuk_ref[...]
        for cp in page_copies(0, 0, 0):
            cp.start()

    @pl.when(j == 0)
    def _():
        wext[nw:nw + nrow_q, :] = qabs_ref[...]
        m_ref[...] = jnp.full(m_ref.shape, NEG_BIG, F32)
        l_ref[...] = jnp.zeros(l_ref.shape, F32)
        acc_ref[...] = jnp.zeros(acc_ref.shape, F32)

    def front(c, kr, cos, sin):
        r = c.shape[0]
        cb = c.astype(BF16)
        nhalf = nw // 2
        big_a = _dot_nt(wext[0:nhalf, :], cb)
        big_b = _dot_nt(wext[nhalf:nw + nrow_q, :], cb)
        krg = kr * jnp.concatenate([kgr_ref[...]] * (r // LANES), axis=1)
        x1, x2 = krg[0:half_r, :], krg[half_r:MLA_ROPE, :]
        krot = jnp.concatenate([x1 * cos - x2 * sin, x2 * cos + x1 * sin], axis=0).astype(BF16)
        s_rope = _dot(qrope_ref[...], krot)
        ssq_r = jnp.sum(kr * kr, axis=0, keepdims=True)
        return cb, big_a, big_b, s_rope, ssq_r

    def back(parts, mask):
        cb, big_a, big_b, s_rope, ssq_r = parts
        r = cb.shape[0]
        nhalf = nw // 2
        hh = MLA_HEADS // 2
        ssq_a = jnp.sum((big_a * big_a).reshape(hh, MLA_NOPE, r), axis=1)
        kn_b = big_b[0:nhalf, :]
        ssq_b = jnp.sum((kn_b * kn_b).reshape(hh, MLA_NOPE, r), axis=1)
        ssq_n = jnp.concatenate([ssq_a, ssq_b], axis=0)
        s_lat = big_b[nhalf:nhalf + nrow_q, :]
        rinv = lax.rsqrt((ssq_n + ssq_r) * (1.0 / MLA_QK) + EPS)
        s = (s_lat + s_rope) * jnp.concatenate([rinv] * ts, axis=0)
        if mask is not None:
            s = jnp.where(mask, s, NEG_BIG)
        m_old = m_ref[...]
        m_new = jnp.maximum(m_old, jnp.max(s, axis=-1, keepdims=True))
        alpha = jnp.exp(m_old - m_new)
        p = jnp.exp(s - m_new)
        l_ref[...] = l_ref[...] * alpha + jnp.sum(p, axis=-1, keepdims=True)
        acc_ref[...] = acc_ref[...] * alpha + _dot(p.astype(BF16), cb)
        m_ref[...] = m_new

    @pl.when(j < nch)
    def _():
        slot = (b * nch + j) % 2
        last_of_seq = j + 1 == nch
        nb = jnp.where(last_of_seq, b + 1, b)
        nj = jnp.where(last_of_seq, 0, j + 1)

        @pl.when(nb < nbatch)
        def _():
            for cp in page_copies(nb, nj, 1 - slot):
                cp.start()

        for cp in page_copies(b, j, slot):
            cp.wait()
        rsub = sub * page

        def chunk_front(g):
            c = latbuf[slot, pl.ds(g * rsub, rsub), :]
            kr = krbuf[slot, :, pl.ds(g * rsub, rsub)]
            base = j * (pps // sub) + g
            cb_ = jnp.concatenate([cosb_ref[base]] * sub, axis=1)
            sb_ = jnp.concatenate([sinb_ref[base]] * sub, axis=1)
            cos = cb_ * loc_ref[0] - sb_ * loc_ref[1]
            sin = sb_ * loc_ref[0] + cb_ * loc_ref[1]
            return front(c, kr, cos, sin)

        ngroups = pps // sub
        parts = chunk_front(0)
        for g in range(ngroups):
            nxt = chunk_front(g + 1) if g + 1 < ngroups else None
            back(parts, None)
            parts = nxt

    @pl.when(j == nch)
    def _():
        r = cnew_ref.shape[0]
        key = lax.broadcasted_iota(jnp.int32, (nrow_q, r), 1)
        tok = lax.broadcasted_iota(jnp.int32, (nrow_q, r), 0) // MLA_HEADS
        back(front(cnew_ref[...], krnew_ref[...], cosn_ref[...], sinn_ref[...]), (key < ts) & (key <= tok))
        o_ref[...] = acc_ref[...] / l_ref[...]


def _sample_attention(layer, cache_latent, cache_k_rope, page_table, cnew, krnew, tabs, wuk_t, qabs, qrope, kgr2,
                      pps, sub):
    bs, n_pages = page_table.shape
    page = cache_latent.shape[2]
    nch = n_pages // pps
    ts = qabs.shape[1] // MLA_HEADS
    cosb, sinb, loc, cosn, sinn = tabs
    pt_flat = page_table.reshape(-1)
    const = lambda a: pl.BlockSpec(a.shape, lambda b, j, pt: (0,) * a.ndim)
    per_seq = lambda a: pl.BlockSpec((None,) + a.shape[1:], lambda b, j, pt: (b,) + (0,) * (a.ndim - 1))
    hbm = pl.BlockSpec(memory_space=pl.ANY)
    in_specs = [hbm, hbm, per_seq(cnew), per_seq(krnew), const(cosb), const(sinb), const(loc),
                const(cosn), const(sinn), const(wuk_t), per_seq(qabs), per_seq(qrope), const(kgr2)]
    nrow_q = qabs.shape[1]
    grid_spec = pltpu.PrefetchScalarGridSpec(
        num_scalar_prefetch=1,
        grid=(bs, nch + 1),
        in_specs=in_specs,
        out_specs=pl.BlockSpec((None, nrow_q, KV_RANK), lambda b, j, pt: (b, 0, 0)),
        scratch_shapes=[pltpu.VMEM((MLA_HEADS * MLA_NOPE + nrow_q, KV_RANK), BF16),
                        pltpu.VMEM((nrow_q, 1), F32), pltpu.VMEM((nrow_q, 1), F32),
                        pltpu.VMEM((nrow_q, KV_RANK), F32),
                        pltpu.VMEM((2, pps * page, KV_RANK), F32),
                        pltpu.VMEM((2, MLA_ROPE, pps * page), F32),
                        pltpu.SemaphoreType.DMA((2,)), pltpu.SemaphoreType.DMA((2,))])
    return pl.pallas_call(
        functools.partial(_sattn_body, layer=layer, pps=pps, sub=sub, nch=nch, ts=ts, n_pages=n_pages, page=page),
        grid_spec=grid_spec,
        out_shape=jax.ShapeDtypeStruct((bs, nrow_q, KV_RANK), F32),
        compiler_params=_cparams(("arbitrary", "arbitrary")),
        name="sample_attention",
    )(pt_flat, cache_latent, cache_k_rope, cnew, krnew, cosb, sinb, loc, cosn, sinn, wuk_t, qabs, qrope, kgr2)


def _uvproj_body(o_ref, w_ref, out_ref):
    out_ref[...] = _dot(o_ref[...].astype(BF16), w_ref[...])


def _uv_project(olat, wuv):
    ns = olat.shape[0]
    return pl.pallas_call(
        _uvproj_body,
        grid=(MLA_HEADS,),
        in_specs=[pl.BlockSpec((ns, KV_RANK), lambda h: (0, h)),
                  pl.BlockSpec((KV_RANK, MLA_V), lambda h: (0, h))],
        out_specs=pl.BlockSpec((ns, MLA_V), lambda h: (0, h)),
        out_shape=jax.ShapeDtypeStruct((ns, MLA_HEADS * MLA_V), F32),
        compiler_params=_cparams(("arbitrary",)),
        name="uv_project",
    )(olat, wuv)


CONV_PAD = 32


def _conv_body(a_ref, g_ref, st_ref, w_ref, b_ref, lng_ref, lnb_ref, o_ref, st_out_ref, ext, *, nseq, t, rc):
    ctx = CONV_WIDTH - 1
    u = a_ref[...] * jax.nn.sigmoid(g_ref[...])
    for s in range(nseq):
        ext[s, 0:SUBLANES, :] = jnp.zeros((SUBLANES, CONV_CH), F32)
        ext[s, CONV_PAD - ctx:CONV_PAD, :] = st_ref[s]
        ext[s, CONV_PAD:CONV_PAD + t, :] = u[s * t:(s + 1) * t, :]
        ext[s, CONV_PAD + t:, :] = jnp.zeros((ext.shape[1] - CONV_PAD - t, CONV_CH), F32)
    bias = b_ref[...]
    lng = lng_ref[...]
    lnb = lnb_ref[...]

    def chunk(s, r0):
        span = ((rc + SUBLANES - 1) // SUBLANES) * SUBLANES
        win = ext[s, pl.ds(r0, span + CONV_PAD + SUBLANES), :]
        acc = jnp.zeros((rc, CONV_CH), F32) + bias
        for b in range(SUBLANES):
            off = CONV_PAD - ctx + b
            sh = win[off:off + span + CONV_PAD - SUBLANES, :]
            for a in range((CONV_WIDTH - b + SUBLANES - 1) // SUBLANES):
                w = SUBLANES * a + b
                acc = acc + sh[SUBLANES * a:SUBLANES * a + rc, :] * w_ref[w:w + 1, :]
        mu = jnp.mean(acc, axis=-1, keepdims=True)
        cen = acc - mu
        var = jnp.mean(cen * cen, axis=-1, keepdims=True)
        y = cen * lax.rsqrt(var + EPS) * lng + lnb
        o_ref[pl.ds(s * t + r0, rc), :] = y * jax.nn.sigmoid(y)

    for s in range(nseq):
        if t // rc == 1:
            chunk(s, 0)
        else:
            def body(i, carry, s=s):
                chunk(s, pl.multiple_of(i * rc, rc))
                return carry
            lax.fori_loop(0, t // rc, body, 0)
        st_out_ref[s] = ext[s, t + CONV_PAD - ctx:t + CONV_PAD, :]


def _conv(z, row0, state, nseq, t, cw, cb, lng, lnb):
    bsz = state.shape[0]
    rows = nseq * t
    r0 = row0 // rows
    rc = min(t, 32)
    const = lambda a: pl.BlockSpec(a.shape, lambda i: (0,) * a.ndim)
    return pl.pallas_call(
        functools.partial(_conv_body, nseq=nseq, t=t, rc=rc),
        grid=(bsz // nseq,),
        in_specs=[pl.BlockSpec((rows, CONV_CH), lambda i: (r0 + i, Z_CA // CONV_CH)),
                  pl.BlockSpec((rows, CONV_CH), lambda i: (r0 + i, Z_CG // CONV_CH)),
                  pl.BlockSpec((nseq, CONV_WIDTH - 1, CONV_CH), lambda i: (i, 0, 0)),
                  const(cw), const(cb), const(lng), const(lnb)],
        out_specs=(pl.BlockSpec((rows, CONV_CH), lambda i: (i, 0)),
                   pl.BlockSpec((nseq, CONV_WIDTH - 1, CONV_CH), lambda i: (i, 0, 0))),
        out_shape=(jax.ShapeDtypeStruct((bsz * t, CONV_CH), F32),
                   jax.ShapeDtypeStruct((bsz, CONV_WIDTH - 1, CONV_CH), F32)),
        scratch_shapes=[pltpu.VMEM((nseq, CONV_PAD + max(t, SUBLANES) + SUBLANES, CONV_CH), F32)],
        compiler_params=_cparams(("arbitrary",)),
        name="conv_module_t%d" % t,
    )(z, z, state, cw, cb, lng, lnb)


def _gla_matrices(rows, seq_len):
    t = np.arange(rows)[:, None]
    u = np.arange(rows)[None, :]
    same = (t // seq_len) == (u // seq_len)
    blocks = [same & (u <= t), same & (u > t)]
    nlev = int(round(math.log2(seq_len)))
    for j in range(nlev):
        h = 2 ** j
        mid = (t // (2 * h)) * (2 * h) + h - 1
        second = (t % (2 * h)) >= h
        blocks.append(np.where(second, (u > mid) & (u <= t), (u > t) & (u <= mid)))
    return np.concatenate(blocks, axis=0).astype(np.float32), nlev


def _gla_body(gq_ref, gk_ref, gv_ref, krl_ref, go_ref, s0_ref, gwp_ref, gb_ref, ng_ref, mall_ref,
              o_ref, sout_ref, st, qd_s, kd_s, ghi_s, glo_s, oi_s, *, seq_len, nlev):
    ci = pl.program_id(1)
    rows = gq_ref.shape[0]
    nh, dk, dv = GLA_HEADS, GLA_DK, GLA_DV
    rg = max(seq_len, SUBLANES)
    spg = rg // seq_len

    @pl.when(ci == 0)
    def _():
        st[...] = s0_ref[...]

    a_hi, a_lo = _split_hi_lo(krl_ref[...])
    w_hi, w_lo = _split_hi_lo(gwp_ref[...])
    x = _dot(a_hi, w_hi) + _dot(a_lo, w_hi) + _dot(a_hi, w_lo) + gb_ref[...]
    g = jax.nn.log_sigmoid(x) * (1.0 / GLA_TAU)
    g_hi, g_lo = _split_hi_lo(g)
    mall = mall_ref[...]
    d = _dot(mall, g_hi) + _dot(mall, g_lo)
    q = gq_ref[...] * (dk ** -0.5)
    k = gk_ref[...]
    v = gv_ref[...].astype(BF16)
    lane_head = lax.broadcasted_iota(jnp.int32, (rows, nh * dk), 1) // dk
    tcol = lax.broadcasted_iota(jnp.int32, (rows, 1), 0)
    rowi = lax.broadcasted_iota(jnp.int32, (rows, rows), 0)
    coli = lax.broadcasted_iota(jnp.int32, (rows, rows), 1)

    def head_stack(a):
        return jnp.concatenate([jnp.where(lane_head == h, a, 0.0) for h in range(nh)], axis=0).astype(BF16)

    p = _dot_nt(head_stack(q), k.astype(BF16))
    amat = [jnp.where(rowi == coli, p[h * rows:(h + 1) * rows], 0.0) for h in range(nh)]
    for j in range(nlev):
        half = 2 ** j
        e = jnp.exp(d[(2 + j) * rows:(3 + j) * rows])
        second = (tcol % (2 * half)) >= half
        qj = jnp.where(second, q * e, 0.0)
        kj = jnp.where(second, 0.0, k * e)
        p = _dot_nt(head_stack(qj), kj.astype(BF16))
        same = (rowi // (2 * half)) == (coli // (2 * half))
        amat = [amat[h] + jnp.where(same, p[h * rows:(h + 1) * rows], 0.0) for h in range(nh)]
    for h in range(nh):
        oi_s[:, h * dv:(h + 1) * dv] = _dot(amat[h].astype(BF16), v[:, h * dv:(h + 1) * dv])

    qd_s[...] = q * jnp.exp(d[0:rows])
    kd_s[...] = k * jnp.exp(d[rows:2 * rows])
    ghi_s[...] = g_hi.astype(F32)
    glo_s[...] = g_lo.astype(F32)

    ones = jnp.ones((rg, dv), BF16)
    gl_head = lax.broadcasted_iota(jnp.int32, (rg, nh * dk), 1) // dk
    g_row = lax.broadcasted_iota(jnp.int32, (rg, 1), 0) // seq_len
    st_head = lax.broadcasted_iota(jnp.int32, (nh * dk, dv), 0) // dk

    def group(gi, carry):
        r0 = pl.multiple_of(gi * rg, rg)
        qd = qd_s[pl.ds(r0, rg), :]
        kd = kd_s[pl.ds(r0, rg), :]
        ghi = ghi_s[pl.ds(r0, rg), :].astype(BF16)
        glo = glo_s[pl.ds(r0, rg), :].astype(BF16)
        vg = gv_ref[pl.ds(r0, rg), :].astype(BF16)
        o_inter = [jnp.zeros((rg, dv), F32) for _ in range(nh)]
        for w in range(spg):
            sidx = gi * spg + w
            s_old = st[sidx]
            s_bf = s_old.astype(BF16)
            mine = g_row == w
            qw = jnp.where(mine, qd, 0.0)
            kw = jnp.where(mine, kd, 0.0).astype(BF16)
            bl = (_dot_tn(jnp.where(mine, ghi, jnp.zeros_like(ghi)), ones)
                  + _dot_tn(jnp.where(mine, glo, jnp.zeros_like(glo)), ones))
            upd = jnp.zeros((nh * dk, dv), F32)
            for h in range(nh):
                o_inter[h] = o_inter[h] + _dot(jnp.where(gl_head == h, qw, 0.0).astype(BF16), s_bf)
                u_h = _dot_tn(kw, vg[:, h * dv:(h + 1) * dv])
                upd = upd + jnp.where(st_head == h, u_h, 0.0)
            st[sidx] = jnp.exp(bl) * s_old + upd
        for h in range(nh):
            oi_s[pl.ds(r0, rg), h * dv:(h + 1) * dv] = oi_s[pl.ds(r0, rg), h * dv:(h + 1) * dv] + o_inter[h]
        return carry

    ngroups = rows // rg
    if ngroups == 1:
        group(0, 0)
    else:
        lax.fori_loop(0, ngroups, group, 0)

    go = go_ref[...]
    for h in range(nh):
        oh = oi_s[:, h * dv:(h + 1) * dv]
        on = oh * lax.rsqrt(jnp.mean(oh * oh, axis=-1, keepdims=True) + EPS) * ng_ref[...]
        gh = go[:, h * dv:(h + 1) * dv]
        o_ref[:, h * dv:(h + 1) * dv] = on * (gh * jax.nn.sigmoid(gh))

    @pl.when(ci == pl.num_programs(1) - 1)
    def _():
        sout_ref[...] = st[...]


def _gla(z, row0, s0, rows, seq_len, nchunks, gwp, gb, ng):
    bsz = s0.shape[0]
    nseq = rows // seq_len if nchunks == 1 else 1
    ngrid = bsz // nseq
    mall_np, nlev = _gla_matrices(rows, seq_len)
    mall = jnp.asarray(mall_np, BF16)
    r0 = row0 // rows
    nk = GLA_HEADS * GLA_DK
    nv = GLA_HEADS * GLA_DV
    zspec = lambda width, col: pl.BlockSpec((rows, width), lambda b, c: (r0 + b * nchunks + c, col // width))
    const = lambda a: pl.BlockSpec(a.shape, lambda b, c: (0,) * a.ndim)
    sspec = pl.BlockSpec((nseq, nk, GLA_DV), lambda b, c: (b, 0, 0))
    return pl.pallas_call(
        functools.partial(_gla_body, seq_len=seq_len, nlev=nlev),
        grid=(ngrid, nchunks),
        in_specs=[zspec(nk, Z_GQ), zspec(nk, Z_GK), zspec(nv, Z_GV), zspec(LANES, Z_KRL), zspec(nv, Z_GO),
                  sspec, const(gwp), const(gb), const(ng), const(mall)],
        out_specs=(pl.BlockSpec((rows, nv), lambda b, c: (b * nchunks + c, 0)), sspec),
        out_shape=(jax.ShapeDtypeStruct((ngrid * nchunks * rows, nv), F32),
                   jax.ShapeDtypeStruct((bsz, nk, GLA_DV), F32)),
        scratch_shapes=[pltpu.VMEM((nseq, nk, GLA_DV), F32),
                        pltpu.VMEM((rows, nk), F32), pltpu.VMEM((rows, nk), F32),
                        pltpu.VMEM((rows, nk), F32), pltpu.VMEM((rows, nk), F32),
                        pltpu.VMEM((rows, nv), F32)],
        compiler_params=_cparams(("arbitrary", "arbitrary")),
        name="gla_l%d" % seq_len,
    )(z, z, z, z, z, s0, gwp, gb, ng, mall)


def _outproj_body(x_ref, mp_ref, cp_ref, gp_ref, ms_ref, cs_ref, gs_ref, w_ref, o_ref, *, np_tiles):
    is_sample = pl.program_id(1) >= np_tiles
    pick = lambda p_ref, s_ref: jnp.where(is_sample, s_ref[...], p_ref[...])
    mix = jnp.concatenate([pick(mp_ref, ms_ref), pick(cp_ref, cs_ref), pick(gp_ref, gs_ref)], axis=1).astype(BF16)
    o_ref[...] = x_ref[...] + _dot(mix, w_ref[...])


def _outproj(x, mixers_p, mixers_s, w, tm):
    n, d = x.shape
    tn = d // 2
    np_tiles = mixers_p[0].shape[0] // tm
    pspec = lambda a: pl.BlockSpec((tm, a.shape[1]), lambda j, i: (jnp.minimum(i, np_tiles - 1), 0))
    sspec = lambda a: pl.BlockSpec((tm, a.shape[1]), lambda j, i: (jnp.maximum(i - np_tiles, 0), 0))
    return pl.pallas_call(
        functools.partial(_outproj_body, np_tiles=np_tiles),
        grid=(2, n // tm),
        in_specs=([pl.BlockSpec((tm, tn), lambda j, i: (i, j))]
                  + [pspec(a) for a in mixers_p] + [sspec(a) for a in mixers_s]
                  + [pl.BlockSpec((w.shape[0], tn), lambda j, i: (0, j))]),
        out_specs=pl.BlockSpec((tm, tn), lambda j, i: (i, j)),
        out_shape=jax.ShapeDtypeStruct((n, d), F32),
        compiler_params=_cparams(("arbitrary", "arbitrary")),
        name="outproj",
    )(x, *mixers_p, *mixers_s, w)


def _router_body(x_ref, g_ref, w_ref, b_ref, h_ref, e_ref, gate_ref):
    x = x_ref[...]
    h = (x * lax.rsqrt(jnp.mean(x * x, axis=-1, keepdims=True) + EPS)) * g_ref[...]
    h_ref[...] = h
    h_hi, h_lo = _split_hi_lo(h)
    w_hi, w_lo = _split_hi_lo(w_ref[...])
    logits = _dot(h_hi, w_hi) + _dot(h_lo, w_hi) + _dot(h_hi, w_lo) + b_ref[...]
    lane = lax.broadcasted_iota(jnp.int32, logits.shape, 1)
    far = jnp.int32(4 * LANES)
    red_max = lambda a: jnp.max(a, axis=-1, keepdims=True)
    red_min = lambda a: jnp.min(a, axis=-1, keepdims=True)
    red_sum = lambda a: jnp.sum(a, axis=-1, keepdims=True)

    is_g = lane < N_GROUPS
    eg = jnp.where(is_g, jnp.exp(logits - red_max(jnp.where(is_g, logits, NEG_BIG))), 0.0)
    pg = eg / red_sum(eg)
    g_prob = red_max(pg)
    g_idx = red_min(jnp.where(is_g & (pg == g_prob), lane, far))
    lo = N_GROUPS + g_idx * EXPERTS_PER_GROUP
    in_g = (lane >= lo) & (lane < lo + EXPERTS_PER_GROUP)
    ee = jnp.where(in_g, jnp.exp(logits - red_max(jnp.where(in_g, logits, NEG_BIG))), 0.0)
    pe = ee / red_sum(ee)
    p1 = red_max(pe)
    i1 = red_min(jnp.where(in_g & (pe == p1), lane, far))
    rest = in_g & (lane != i1)
    p2 = red_max(jnp.where(rest, pe, -1.0))
    i2 = red_min(jnp.where(rest & (pe == p2), lane, far))
    den = p1 + p2
    e_ref[...] = jnp.where(lane == 0, i1 - N_GROUPS, jnp.where(lane == 1, i2 - N_GROUPS, 0))
    gate_ref[...] = jnp.where(lane == 0, g_prob * p1 / den, jnp.where(lane == 1, g_prob * p2 / den, 0.0))


def _router(x, gain, w, b, tm):
    n, d = x.shape
    const = lambda a: pl.BlockSpec(a.shape, lambda i: (0,) * a.ndim)
    return pl.pallas_call(
        _router_body,
        grid=(n // tm,),
        in_specs=[pl.BlockSpec((tm, d), lambda i: (i, 0)), const(gain), const(w), const(b)],
        out_specs=(pl.BlockSpec((tm, d), lambda i: (i, 0)),
                   pl.BlockSpec((tm, LANES), lambda i: (i, 0)),
                   pl.BlockSpec((tm, LANES), lambda i: (i, 0))),
        out_shape=(jax.ShapeDtypeStruct((n, d), F32),
                   jax.ShapeDtypeStruct((n, LANES), jnp.int32),
                   jax.ShapeDtypeStruct((n, LANES), F32)),
        compiler_params=_cparams(("arbitrary",)),
        name="moe_router",
    )(x, gain, w, b)


def _ffn_body(bexp_ref, nval_ref, rtok_ref, rout_ref, h_hbm, gate_ref, wg_ref, wu_ref, wd_ref, oa_hbm,
              xbuf, obuf, wgb, wub, wdb, gsem, ssem, *, bm):
    i = pl.program_id(0)
    nb = pl.num_programs(0)
    nv = nval_ref[i]
    slot = i % 2
    nxt = jnp.minimum(i + 1, nb - 1)
    has_next = (i + 1 < nb) & (nval_ref[nxt] > 0)

    def gather_copy(step, r, sl):
        return pltpu.make_async_copy(h_hbm.at[rtok_ref[step * bm + r]], xbuf.at[sl, r], gsem.at[sl])

    def scatter_copy(step, r):
        return pltpu.make_async_copy(obuf.at[r], oa_hbm.at[rout_ref[step * bm + r]], ssem)

    def start_gather(step, sl):
        def body(r, c):
            gather_copy(step, r, sl).start()
            return c
        lax.fori_loop(0, bm, body, 0, unroll=8)

    def wait_gather(step, sl):
        def body(r, c):
            gather_copy(step, r, sl).wait()
            return c
        lax.fori_loop(0, bm, body, 0, unroll=8)

    def wait_scatter(step):
        def body(r, c):
            scatter_copy(step, r).wait()
            return c
        lax.fori_loop(0, nval_ref[step], body, 0)

    @pl.when(nv > 0)
    def _():
        @pl.when(i == 0)
        def _():
            start_gather(0, 0)

        @pl.when(has_next)
        def _():
            start_gather(nxt, 1 - slot)

        prev = bexp_ref[jnp.maximum(i - 1, 0)]
        @pl.when((i == 0) | (bexp_ref[i] != prev))
        def _():
            wgb[...] = wg_ref[...].astype(BF16)
            wub[...] = wu_ref[...].astype(BF16)
            wdb[...] = wd_ref[...].astype(BF16)

        wait_gather(i, slot)
        x = xbuf[slot].astype(BF16)
        a = _dot_nt(x, wgb[...])
        u = _dot_nt(x, wub[...])
        mid = (a * jax.nn.sigmoid(a) * u).astype(BF16)
        out = _dot(mid, wdb[...]) * gate_ref[...]

        @pl.when(i > 0)
        def _():
            wait_scatter(i - 1)

        obuf[...] = out

        def sstart(r, c):
            scatter_copy(i, r).start()
            return c
        lax.fori_loop(0, nv, sstart, 0)

        @pl.when(jnp.logical_not(has_next))
        def _():
            wait_scatter(i)


def _expert_ffn(layer, h, block_exp, block_nvalid, row_tok, row_out, row_gate, wg_t, wu_t, w_down, bm):
    n, d = h.shape
    nb = block_exp.shape[0]
    de = w_down.shape[2]
    wspec = pl.BlockSpec((None, None, de, d), lambda i, be, nv, rt, ro: (layer, be[i], 0, 0))
    grid_spec = pltpu.PrefetchScalarGridSpec(
        num_scalar_prefetch=4,
        grid=(nb,),
        in_specs=[pl.BlockSpec(memory_space=pl.ANY),
                  pl.BlockSpec((bm, 1), lambda i, be, nv, rt, ro: (i, 0)),
                  wspec, wspec, wspec],
        out_specs=pl.BlockSpec(memory_space=pl.ANY),
        scratch_shapes=[pltpu.VMEM((2, bm, d), F32), pltpu.VMEM((bm, d), F32),
                        pltpu.VMEM((de, d), BF16), pltpu.VMEM((de, d), BF16), pltpu.VMEM((de, d), BF16),
                        pltpu.SemaphoreType.DMA((2,)), pltpu.SemaphoreType.DMA(())])
    return pl.pallas_call(
        functools.partial(_ffn_body, bm=bm),
        grid_spec=grid_spec,
        out_shape=jax.ShapeDtypeStruct((TOP_K * n, d), F32),
        compiler_params=pltpu.CompilerParams(dimension_semantics=("arbitrary",),
                                             vmem_limit_bytes=FFN_VMEM_LIMIT_BYTES),
        name="expert_ffn",
    )(block_exp, block_nvalid, row_tok, row_out, h, row_gate, wg_t, wu_t, w_down)


def _combine_body(x_ref, a_ref, b_ref, o_ref):
    o_ref[...] = x_ref[...] + (a_ref[...] + b_ref[...])


def _combine(x, oa, tm):
    n, d = x.shape
    nt = n // tm
    return pl.pallas_call(
        _combine_body,
        grid=(nt,),
        in_specs=[pl.BlockSpec((tm, d), lambda i: (i, 0)), pl.BlockSpec((tm, d), lambda i: (i, 0)),
                  pl.BlockSpec((tm, d), lambda i: (nt + i, 0))],
        out_specs=pl.BlockSpec((tm, d), lambda i: (i, 0)),
        out_shape=jax.ShapeDtypeStruct((n, d), F32),
        compiler_params=_cparams(("arbitrary",)),
        name="moe_combine",
    )(x, oa, oa)


def _dispatch_plan(experts, gates, bm):
    n = experts.shape[0]
    a = n * TOP_K
    e_flat = experts.reshape(a)
    g_flat = gates.reshape(a)
    onehot = (e_flat[:, None] == jnp.arange(N_EXPERTS, dtype=jnp.int32)[None, :]).astype(jnp.int32)
    csum = jnp.cumsum(onehot, axis=0)
    counts = csum[-1]
    rank = jnp.take_along_axis(csum, e_flat[:, None], axis=1)[:, 0] - 1
    padded = (counts + bm - 1) // bm * bm
    pad_end = jnp.cumsum(padded)
    pad_start = pad_end - padded
    dest = pad_start[e_flat] + rank
    nb = -(-(a + N_EXPERTS * (bm - 1)) // bm)
    rows = nb * bm
    row_asg = jnp.full((rows,), -1, jnp.int32).at[dest].set(jnp.arange(a, dtype=jnp.int32))
    valid = row_asg >= 0
    safe = jnp.maximum(row_asg, 0)
    row_tok = jnp.where(valid, safe // TOP_K, 0).astype(jnp.int32)
    row_gate = jnp.where(valid, g_flat[safe], 0.0).astype(F32)[:, None]
    starts = jnp.arange(nb, dtype=jnp.int32) * bm
    block_exp = jnp.minimum(jnp.sum((pad_end[None, :] <= starts[:, None]).astype(jnp.int32), axis=1),
                            N_EXPERTS - 1).astype(jnp.int32)
    block_nvalid = jnp.clip(counts[block_exp] - (starts - pad_start[block_exp]), 0, bm).astype(jnp.int32)
    block_nvalid = jnp.where(starts < pad_end[-1], block_nvalid, 0)
    row_out = ((safe % TOP_K) * n + safe // TOP_K).astype(jnp.int32)
    return block_exp, block_nvalid, row_tok, row_out, row_gate


FFN_BLOCK_ROWS = 256


def kernel(x_prompt, x_sample, cache_latent, cache_k_rope, state_conv, state_gla, page_table, ln_mix, w_in,
           latent_gain, q_gain, k_gain, w_uk, w_uv, conv_w, conv_b, conv_ln_g, conv_ln_b, gla_gate_w, gla_gate_b,
           gla_norm_g, w_out, ln_ffn, router_group_w, router_group_b, router_expert_w, router_expert_b,
           w_gate, w_up, w_down):
    bp, sp, d = x_prompt.shape
    bs, ts, _ = x_sample.shape
    depth = w_in.shape[0]
    n_p, n_s = bp * sp, bs * ts
    n = n_p + n_s
    n_pages = page_table.shape[1]
    page = cache_latent.shape[2]
    past = n_pages * page
    tm = math.gcd(math.gcd(n_p, n_s), 512)
    tmb = math.gcd(tm, 256)
    tq = math.gcd(sp, 256)
    pps = min(SATTN_PPS, n_pages)
    sub = min(SATTN_SUB, pps)
    assert n_pages % pps == 0 and pps % sub == 0 and sp % tmb == 0 and tmb % ts == 0
    gla_rows = min(128, sp)
    gla_rows_s = min(128, n_s)
    assert sp % gla_rows == 0 and n_s % gla_rows_s == 0 and gla_rows_s % max(ts, SUBLANES) == 0
    assert n_p % gla_rows_s == 0 and (ts & (ts - 1)) == 0
    conv_nseq = SUBLANES
    assert bs % conv_nseq == 0 and n_p % (conv_nseq * ts) == 0

    cos_p, sin_p = _rope_tables(np.arange(sp))
    cos_s, sin_s = _rope_tables(past + (np.arange(tmb) % ts))
    tabs = _sample_rope_tables(past, page, sub * page)
    cos_p, sin_p, cos_s, sin_s = (jnp.asarray(t) for t in (cos_p, sin_p, cos_s, sin_s))
    kr_cache_t = jnp.swapaxes(cache_k_rope, 2, 3)
    wg_t = jnp.swapaxes(w_gate, 2, 3)
    wu_t = jnp.swapaxes(w_up, 2, 3)
    w_in_t = jnp.swapaxes(w_in, 1, 2)

    src = _z_source_columns()
    src_idx = jnp.asarray(np.maximum(src, 0), jnp.int32)
    src_ok = jnp.asarray(src >= 0)

    x = jnp.concatenate([x_prompt.reshape(n_p, d), x_sample.reshape(n_s, d)], axis=0)
    zeros_conv = jnp.zeros((bp, CONV_WIDTH - 1, CONV_CH), F32)
    zeros_gla = jnp.zeros((bp, GLA_HEADS * GLA_DK, GLA_DV), F32)
    outs = [[] for _ in range(8)]

    for l in range(depth):
        row2 = lambda a: a.reshape(1, -1).astype(F32)
        w_in_p = jnp.where(src_ok[:, None], jnp.take(w_in_t[l], src_idx, axis=0), 0.0).astype(BF16)
        wuk = w_uk[l].reshape(KV_RANK, MLA_HEADS * MLA_NOPE).astype(BF16)
        wuv = w_uv[l].reshape(KV_RANK, MLA_HEADS * MLA_V).astype(BF16)
        qgn, kgn = row2(q_gain[l, :MLA_NOPE]), row2(k_gain[l, :MLA_NOPE])
        qgr2 = row2(jnp.tile(q_gain[l, MLA_NOPE:], 2))
        kgr2 = row2(jnp.tile(k_gain[l, MLA_NOPE:], 2))
        lg = row2(latent_gain[l])

        z = _inproj(x, row2(ln_mix[l]), w_in_p, tm)

        c_p, q_p, k_p, v_p = _prep(z, 0, n_p, tmb, cos_p, sin_p, False, lg, qgn, qgr2, kgn, kgr2, wuk, wuv)
        c_s, qrope_s, qabs_s = _prep(z, n_p, n_s, tmb, cos_s, sin_s, True, lg, qgn, qgr2, kgn, kgr2, wuk, wuv)
        mla_p = _prompt_attention(q_p, k_p, v_p, bp, sp, tq)
        kr_all = z[:, Z_KRL:Z_KRL + MLA_ROPE]
        cnew = jnp.pad(c_s.reshape(bs, ts, KV_RANK), ((0, 0), (0, page - ts), (0, 0)))
        krnew = jnp.pad(jnp.swapaxes(kr_all[n_p:].reshape(bs, ts, MLA_ROPE), 1, 2),
                        ((0, 0), (0, 0), (0, page - ts)))
        kg_lanes = jnp.broadcast_to(k_gain[l, MLA_NOPE:].astype(F32)[:, None], (MLA_ROPE, LANES))
        o_lat = _sample_attention(l, cache_latent, kr_cache_t, page_table, cnew, krnew, tabs, wuk.T,
                                  qabs_s.reshape(bs, ts * MLA_HEADS, KV_RANK),
                                  qrope_s.reshape(bs, ts * MLA_HEADS, MLA_ROPE), kg_lanes, pps, sub)
        mla_s = _uv_project(o_lat.reshape(n_s, MLA_HEADS * KV_RANK), wuv)

        cw, cb = conv_w[l].astype(F32), row2(conv_b[l])
        clg, clb = row2(conv_ln_g[l]), row2(conv_ln_b[l])
        conv_p, cst_p = _conv(z, 0, zeros_conv, 1, sp, cw, cb, clg, clb)
        conv_s, cst_s = _conv(z, n_p, state_conv[l], conv_nseq, ts, cw, cb, clg, clb)

        gwp = jnp.zeros((LANES, GLA_HEADS * GLA_DK), F32).at[MLA_ROPE:MLA_ROPE + GLA_GATE_RANK].set(gla_gate_w[l])
        ggb, gng = row2(gla_gate_b[l]), row2(gla_norm_g[l])
        gla_p, gst_p = _gla(z, 0, zeros_gla, gla_rows, gla_rows, sp // gla_rows, gwp, ggb, gng)
        gla_s, gst_s = _gla(z, n_p, state_gla[l].reshape(bs, GLA_HEADS * GLA_DK, GLA_DV), gla_rows_s, ts, 1,
                            gwp, ggb, gng)

        x = _outproj(x, (mla_p, conv_p, gla_p), (mla_s, conv_s, gla_s), w_out[l].astype(BF16), tm)

        wr = jnp.zeros((d, LANES), F32)
        wr = wr.at[:, :N_GROUPS].set(router_group_w[l]).at[:, N_GROUPS:N_GROUPS + N_EXPERTS].set(router_expert_w[l])
        br = jnp.zeros((1, LANES), F32)
        br = br.at[0, :N_GROUPS].set(router_group_b[l]).at[0, N_GROUPS:N_GROUPS + N_EXPERTS].set(router_expert_b[l])
        h, e_idx, gate = _router(x, row2(ln_ffn[l]), wr, br, tm)
        plan = _dispatch_plan(e_idx[:, :TOP_K], gate[:, :TOP_K], FFN_BLOCK_ROWS)
        oa = _expert_ffn(l, h, *plan, wg_t, wu_t, w_down, FFN_BLOCK_ROWS)
        x = _combine(x, oa, tm)

        outs[0].append(c_p.reshape(bp, sp, KV_RANK))
        outs[1].append(kr_all[:n_p].reshape(bp, sp, MLA_ROPE))
        outs[2].append(cst_p)
        outs[3].append(gst_p.reshape(bp, GLA_HEADS, GLA_DK, GLA_DV))
        outs[4].append(c_s.reshape(bs, ts, KV_RANK))
        outs[5].append(kr_all[n_p:].reshape(bs, ts, MLA_ROPE))
        outs[6].append(cst_s)
        outs[7].append(gst_s.reshape(bs, GLA_HEADS, GLA_DK, GLA_DV))

    return (x[:n_p].reshape(bp, sp, d), x[n_p:].reshape(bs, ts, d)) + tuple(jnp.stack(o) for o in outs)
```

```python
import functools
import math

import numpy as np
import jax
import jax.numpy as jnp
from jax import lax
from jax.experimental import pallas as pl
from jax.experimental.pallas import tpu as pltpu

F32 = jnp.float32
BF16 = jnp.bfloat16

MLA_HEADS = 8
MLA_NOPE = 128
MLA_ROPE = 64
MLA_V = 128
MLA_QK = MLA_NOPE + MLA_ROPE
KV_RANK = 512
ROPE_BASE = 10000.0
CONV_CH = 512
CONV_WIDTH = 31
GLA_HEADS = 4
GLA_DV = 128
GLA_DK = 64
GLA_GATE_RANK = 16
GLA_TAU = 16.0
N_GROUPS = 4
EXPERTS_PER_GROUP = 8
N_EXPERTS = N_GROUPS * EXPERTS_PER_GROUP
TOP_K = 2
D_EXPERT = 704
EPS = 1e-6
NEG_BIG = -1e30

LANES = 128
SUBLANES = 8
VMEM_LIMIT_BYTES = 56 * 1024 * 1024
FFN_VMEM_LIMIT_BYTES = 60 * 1024 * 1024

Z_QN = 0
Z_QR = 1024
Z_C = 1536
Z_CA = 2048
Z_CG = 2560
Z_GQ = 3072
Z_GK = 3328
Z_GV = 3584
Z_GO = 4096
Z_KRL = 4608
Z_COLS = 4864


def _z_source_columns():
    src = -np.ones((Z_COLS,), np.int64)
    q0 = 0
    for h in range(MLA_HEADS):
        src[Z_QN + h * MLA_NOPE:Z_QN + (h + 1) * MLA_NOPE] = q0 + h * MLA_QK + np.arange(MLA_NOPE)
        src[Z_QR + h * MLA_ROPE:Z_QR + (h + 1) * MLA_ROPE] = q0 + h * MLA_QK + MLA_NOPE + np.arange(MLA_ROPE)
    o = MLA_HEADS * MLA_QK
    src[Z_C:Z_C + KV_RANK] = o + np.arange(KV_RANK); o += KV_RANK
    src[Z_KRL:Z_KRL + MLA_ROPE] = o + np.arange(MLA_ROPE); o += MLA_ROPE
    src[Z_CA:Z_CA + CONV_CH] = o + np.arange(CONV_CH); o += CONV_CH
    src[Z_CG:Z_CG + CONV_CH] = o + np.arange(CONV_CH); o += CONV_CH
    n = GLA_HEADS * GLA_DK
    src[Z_GQ:Z_GQ + n] = o + np.arange(n); o += n
    src[Z_GK:Z_GK + n] = o + np.arange(n); o += n
    n = GLA_HEADS * GLA_DV
    src[Z_GV:Z_GV + n] = o + np.arange(n); o += n
    src[Z_KRL + MLA_ROPE:Z_KRL + MLA_ROPE + GLA_GATE_RANK] = o + np.arange(GLA_GATE_RANK); o += GLA_GATE_RANK
    src[Z_GO:Z_GO + n] = o + np.arange(n); o += n
    return src


def _cparams(sem):
    return pltpu.CompilerParams(dimension_semantics=sem, vmem_limit_bytes=VMEM_LIMIT_BYTES)


def _dot(a, b):
    return jnp.dot(a, b, preferred_element_type=F32)


def _dot_nt(a, b):
    return lax.dot_general(a, b, (((1,), (1,)), ((), ())), preferred_element_type=F32)


def _dot_tn(a, b):
    return lax.dot_general(a, b, (((0,), (0,)), ((), ())), preferred_element_type=F32)


def _split_hi_lo(a):
    hi = a.astype(BF16)
    lo = (a - hi.astype(F32)).astype(BF16)
    return hi, lo


def _rope_tables(pos, period_lanes=LANES):
    half = MLA_ROPE // 2
    inv = ROPE_BASE ** (-np.arange(half, dtype=np.float64) / half)
    lane = np.arange(period_lanes)
    ang = np.asarray(pos, np.float64)[:, None] * inv[lane % half][None, :]
    sign = np.where((lane % MLA_ROPE) < half, -1.0, 1.0)[None, :]
    return np.cos(ang).astype(np.float32), (np.sin(ang) * sign).astype(np.float32)


def _inproj_body(x_ref, g_ref, w_ref, o_ref):
    x = x_ref[...]
    ms = jnp.mean(x * x, axis=-1, keepdims=True)
    xn = (x * lax.rsqrt(ms + EPS)) * g_ref[...]
    o_ref[...] = _dot_nt(xn.astype(BF16), w_ref[...])


def _inproj(x, gain, w, tm):
    n, d = x.shape
    zc = w.shape[0]
    tn = zc // 2
    return pl.pallas_call(
        _inproj_body,
        grid=(2, n // tm),
        in_specs=[pl.BlockSpec((tm, d), lambda j, i: (i, 0)),
                  pl.BlockSpec((1, d), lambda j, i: (0, 0)),
                  pl.BlockSpec((tn, d), lambda j, i: (j, 0))],
        out_specs=pl.BlockSpec((tm, tn), lambda j, i: (i, j)),
        out_shape=jax.ShapeDtypeStruct((n, zc), F32),
        compiler_params=_cparams(("arbitrary", "arbitrary")),
        name="inproj",
    )(x, gain, w)


def _swap32(x):
    lane = lax.broadcasted_iota(jnp.int32, x.shape, x.ndim - 1)
    return jnp.where((lane % MLA_ROPE) < MLA_ROPE // 2, pltpu.roll(x, 96, x.ndim - 1), pltpu.roll(x, 32, x.ndim - 1))


def _prep_body(sample, qn_ref, qr_ref, c_ref, krl_ref, cos_ref, sin_ref, lg_ref, qgn_ref, qgr_ref,
               kgn_ref, kgr_ref, wuk_ref, wuv_ref, *outs):
    if sample:
        c_out, qrope_out, qabs_out = outs
    else:
        c_out, q_out, k_out, v_out = outs
    scale = MLA_QK ** -0.5
    cos = cos_ref[...]
    sin = sin_ref[...]
    tm = cos.shape[0]
    lane = lax.broadcasted_iota(jnp.int32, (tm, LANES), 1)
    low_half = lane < MLA_ROPE

    c_raw = c_ref[...]
    c = c_raw * lax.rsqrt(jnp.mean(c_raw * c_raw, axis=-1, keepdims=True) + EPS) * lg_ref[...]
    c_out[...] = c

    qn = qn_ref[...]
    qr = qr_ref[...]
    for hp in range(MLA_HEADS // 2):
        pair = qr[:, hp * LANES:(hp + 1) * LANES]
        pair_sq = pair * pair
        rinv = []
        for h in (2 * hp, 2 * hp + 1):
            qh = qn[:, h * MLA_NOPE:(h + 1) * MLA_NOPE]
            half_sq = jnp.where(low_half if h % 2 == 0 else ~low_half, pair_sq, 0.0)
            ssq = jnp.sum(qh * qh, axis=-1, keepdims=True) + jnp.sum(half_sq, axis=-1, keepdims=True)
            rinv.append(lax.rsqrt(ssq * (1.0 / MLA_QK) + EPS))
        pair_n = pair * jnp.where(low_half, rinv[0], rinv[1]) * qgr_ref[...]
        rot = (pair_n * cos + _swap32(pair_n) * sin) * scale
        if sample:
            qrope_out[:, hp * LANES:(hp + 1) * LANES] = rot.astype(BF16)
        for k, h in enumerate((2 * hp, 2 * hp + 1)):
            qh = qn[:, h * MLA_NOPE:(h + 1) * MLA_NOPE] * rinv[k]
            rot_h = jnp.where(low_half if k == 0 else ~low_half, rot, 0.0)
            if sample:
                qa = (qh * (qgn_ref[...] * kgn_ref[...] * scale)).astype(BF16)
                qabs_out[:, h * KV_RANK:(h + 1) * KV_RANK] = _dot_nt(
                    qa, wuk_ref[:, h * MLA_NOPE:(h + 1) * MLA_NOPE]).astype(BF16)
            else:
                q_out[:, h * 256:h * 256 + LANES] = (qh * (qgn_ref[...] * scale)).astype(BF16)
                q_out[:, h * 256 + LANES:(h + 1) * 256] = rot_h.astype(BF16)

    if not sample:
        cb = c.astype(BF16)
        kn = _dot(cb, wuk_ref[...])
        v_out[...] = _dot(cb, wuv_ref[...]).astype(BF16)
        krl = krl_ref[...]
        kr2 = jnp.where(low_half, krl, pltpu.roll(krl, MLA_ROPE, 1))
        ssq_r = jnp.sum(jnp.where(low_half, krl * krl, 0.0), axis=-1, keepdims=True)
        krg = kr2 * kgr_ref[...]
        krot = krg * cos + pltpu.roll(krg, MLA_ROPE // 2, 1) * sin
        for h in range(MLA_HEADS):
            kh = kn[:, h * MLA_NOPE:(h + 1) * MLA_NOPE]
            ssq = jnp.sum(kh * kh, axis=-1, keepdims=True) + ssq_r
            rinv_k = lax.rsqrt(ssq * (1.0 / MLA_QK) + EPS)
            k_out[:, h * 256:h * 256 + LANES] = (kh * rinv_k * kgn_ref[...]).astype(BF16)
            k_out[:, h * 256 + LANES:(h + 1) * 256] = jnp.where(
                low_half if h % 2 == 0 else ~low_half, krot * rinv_k, 0.0).astype(BF16)


def _prep(z, row0, nrows, tm, cos, sin, sample, lg, qgn, qgr2, kgn, kgr2, wuk, wuv):
    r0 = row0 // tm
    ptiles = cos.shape[0] // tm
    zspec = lambda width, col: pl.BlockSpec((tm, width), lambda i: (r0 + i, col // width))
    const = lambda a: pl.BlockSpec(a.shape, lambda i: (0,) * a.ndim)
    tab = pl.BlockSpec((tm, LANES), lambda i: (i % ptiles, 0))
    in_specs = [zspec(1024, Z_QN), zspec(512, Z_QR), zspec(512, Z_C), zspec(LANES, Z_KRL), tab, tab,
                const(lg), const(qgn), const(qgr2), const(kgn), const(kgr2), const(wuk), const(wuv)]
    row = lambda width: pl.BlockSpec((tm, width), lambda i: (i, 0))
    if sample:
        out_shape = (jax.ShapeDtypeStruct((nrows, KV_RANK), F32),
                     jax.ShapeDtypeStruct((nrows, MLA_HEADS * MLA_ROPE), BF16),
                     jax.ShapeDtypeStruct((nrows, MLA_HEADS * KV_RANK), BF16))
        out_specs = (row(KV_RANK), row(MLA_HEADS * MLA_ROPE), row(MLA_HEADS * KV_RANK))
    else:
        out_shape = (jax.ShapeDtypeStruct((nrows, KV_RANK), F32),
                     jax.ShapeDtypeStruct((nrows, MLA_HEADS * 256), BF16),
                     jax.ShapeDtypeStruct((nrows, MLA_HEADS * 256), BF16),
                     jax.ShapeDtypeStruct((nrows, MLA_HEADS * MLA_V), BF16))
        out_specs = (row(KV_RANK), row(MLA_HEADS * 256), row(MLA_HEADS * 256), row(MLA_HEADS * MLA_V))
    return pl.pallas_call(
        functools.partial(_prep_body, sample),
        grid=(nrows // tm,),
        in_specs=in_specs, out_specs=out_specs, out_shape=out_shape,
        compiler_params=_cparams(("arbitrary",)),
        name="mla_prep_sample" if sample else "mla_prep_prompt",
    )(z, z, z, z, cos, sin, lg, qgn, qgr2, kgn, kgr2, wuk, wuv)


def _pattn_body(q_ref, k_ref, v_ref, o_ref, *, tq):
    sp = q_ref.shape[0]
    row = lax.broadcasted_iota(jnp.int32, (tq, tq), 0)
    col = lax.broadcasted_iota(jnp.int32, (tq, tq), 1)
    for qi in range(sp // tq):
        q = q_ref[qi * tq:(qi + 1) * tq, :]
        s_d = jnp.where(col <= row, _dot_nt(q, k_ref[qi * tq:(qi + 1) * tq, :]), NEG_BIG)
        m = jnp.max(s_d, axis=-1, keepdims=True)
        if qi > 0:
            s_p = _dot_nt(q, k_ref[0:qi * tq, :])
            m = jnp.maximum(m, jnp.max(s_p, axis=-1, keepdims=True))
        p_d = jnp.exp(s_d - m)
        l = jnp.sum(p_d, axis=-1, keepdims=True)
        acc = _dot(p_d.astype(BF16), v_ref[qi * tq:(qi + 1) * tq, :])
        if qi > 0:
            p_p = jnp.exp(s_p - m)
            l = l + jnp.sum(p_p, axis=-1, keepdims=True)
            acc = acc + _dot(p_p.astype(BF16), v_ref[0:qi * tq, :])
        o_ref[qi * tq:(qi + 1) * tq, :] = acc / l


def _prompt_attention(q, k, v, bp, sp, tq):
    return pl.pallas_call(
        functools.partial(_pattn_body, tq=tq),
        grid=(bp, MLA_HEADS),
        in_specs=[pl.BlockSpec((sp, 256), lambda b, h: (b, h)),
                  pl.BlockSpec((sp, 256), lambda b, h: (b, h)),
                  pl.BlockSpec((sp, MLA_V), lambda b, h: (b, h))],
        out_specs=pl.BlockSpec((sp, MLA_V), lambda b, h: (b, h)),
        out_shape=jax.ShapeDtypeStruct((bp * sp, MLA_HEADS * MLA_V), F32),
        compiler_params=_cparams(("arbitrary", "arbitrary")),
        name="prompt_attention",
    )(q, k, v)


SATTN_PPS = 64
SATTN_SUB = 8


def _sample_rope_tables(past, page, rsub):
    half = MLA_ROPE // 2
    inv = (ROPE_BASE ** (-np.arange(half, dtype=np.float64) / half))[:, None]
    ang_b = inv[None] * (np.arange(past // rsub, dtype=np.float64) * rsub)[:, None, None]
    ang_b = np.broadcast_to(ang_b, (past // rsub, half, LANES))
    ang_l = inv * np.arange(rsub, dtype=np.float64)[None, :]
    ang_n = inv * (past + np.arange(page, dtype=np.float64))[None, :]
    f = lambda a: jnp.asarray(a.astype(np.float32))
    return (f(np.cos(ang_b)), f(np.sin(ang_b)), f(np.stack([np.cos(ang_l), np.sin(ang_l)])),
            f(np.cos(ang_n)), f(np.sin(ang_n)))


def _sattn_body(pt_ref, lat_hbm, kr_hbm, cnew_ref, krnew_ref, cosb_ref, sinb_ref, loc_ref, cosn_ref, sinn_ref,
                wuk_ref, qabs_ref, qrope_ref, kgr_ref, o_ref, wext, m_ref, l_ref, acc_ref, latbuf, krbuf,
                lsem, ksem, *, layer, pps, sub, nch, ts, n_pages, page):
    b = pl.program_id(0)
    j = pl.program_id(1)
    nbatch = pl.num_programs(0)
    nrow_q = qabs_ref.shape[0]
    nw = MLA_HEADS * MLA_NOPE
    half_r = MLA_ROPE // 2

    def page_copies(bb, jj, slot):
        out = []
        for i in range(pps):
            pg = pt_ref[bb * n_pages + jj * pps + i]
            out.append(pltpu.make_async_copy(lat_hbm.at[layer, pg], latbuf.at[slot, pl.ds(i * page, page)],
                                             lsem.at[slot]))
            out.append(pltpu.make_async_copy(kr_hbm.at[layer, pg], krbuf.at[slot, :, pl.ds(i * page, page)],
                                             ksem.at[slot]))
        return out

    @pl.when((b == 0) & (j == 0))
    def _():
        wext[0:nw, :] = wuk_ref[...]
        for cp in page_copies(0, 0, 0):
            cp.start()

    @pl.when(j == 0)
    def _():
        wext[nw:nw + nrow_q, :] = qabs_ref[...]
        m_ref[...] = jnp.full(m_ref.shape, NEG_BIG, F32)
        l_ref[...] = jnp.zeros(l_ref.shape, F32)
        acc_ref[...] = jnp.zeros(acc_ref.shape, F32)

    def front(c, kr, cos, sin):
        r = c.shape[0]
        cb = c.astype(BF16)
        nhalf = nw // 2
        big_a = _dot_nt(wext[0:nhalf, :], cb)
        big_b = _dot_nt(wext[nhalf:nw + nrow_q, :], cb)
        krg = kr * jnp.concatenate([kgr_ref[...]] * (r // LANES), axis=1)
        x1, x2 = krg[0:half_r, :], krg[half_r:MLA_ROPE, :]
        krot = jnp.concatenate([x1 * cos - x2 * sin, x2 * cos + x1 * sin], axis=0).astype(BF16)
        s_rope = _dot(qrope_ref[...], krot)
        ssq_r = jnp.sum(kr * kr, axis=0, keepdims=True)
        return cb, big_a, big_b, s_rope, ssq_r

    def back(parts, mask):
        cb, big_a, big_b, s_rope, ssq_r = parts
        r = cb.shape[0]
        nhalf = nw // 2
        hh = MLA_HEADS // 2
        ssq_a = jnp.sum((big_a * big_a).reshape(hh, MLA_NOPE, r), axis=1)
        kn_b = big_b[0:nhalf, :]
        ssq_b = jnp.sum((kn_b * kn_b).reshape(hh, MLA_NOPE, r), axis=1)
        ssq_n = jnp.concatenate([ssq_a, ssq_b], axis=0)
        s_lat = big_b[nhalf:nhalf + nrow_q, :]
        rinv = lax.rsqrt((ssq_n + ssq_r) * (1.0 / MLA_QK) + EPS)
        s = (s_lat + s_rope) * jnp.concatenate([rinv] * ts, axis=0)
        if mask is not None:
            s = jnp.where(mask, s, NEG_BIG)
        m_old = m_ref[...]
        m_new = jnp.maximum(m_old, jnp.max(s, axis=-1, keepdims=True))
        alpha = jnp.exp(m_old - m_new)
        p = jnp.exp(s - m_new)
        l_ref[...] = l_ref[...] * alpha + jnp.sum(p, axis=-1, keepdims=True)
        acc_ref[...] = acc_ref[...] * alpha + _dot(p.astype(BF16), cb)
        m_ref[...] = m_new

    @pl.when(j < nch)
    def _():
        slot = (b * nch + j) % 2
        last_of_seq = j + 1 == nch
        nb = jnp.where(last_of_seq, b + 1, b)
        nj = jnp.where(last_of_seq, 0, j + 1)

        @pl.when(nb < nbatch)
        def _():
            for cp in page_copies(nb, nj, 1 - slot):
                cp.start()

        for cp in page_copies(b, j, slot):
            cp.wait()
        rsub = sub * page

        def chunk_front(g):
            c = latbuf[slot, pl.ds(g * rsub, rsub), :]
            kr = krbuf[slot, :, pl.ds(g * rsub, rsub)]
            base = j * (pps // sub) + g
            cb_ = jnp.concatenate([cosb_ref[base]] * sub, axis=1)
            sb_ = jnp.concatenate([sinb_ref[base]] * sub, axis=1)
            cos = cb_ * loc_ref[0] - sb_ * loc_ref[1]
            sin = sb_ * loc_ref[0] + cb_ * loc_ref[1]
            return front(c, kr, cos, sin)

        ngroups = pps // sub
        parts = chunk_front(0)
        for g in range(ngroups):
            nxt = chunk_front(g + 1) if g + 1 < ngroups else None
            back(parts, None)
            parts = nxt

    @pl.when(j == nch)
    def _():
        r = cnew_ref.shape[0]
        key = lax.broadcasted_iota(jnp.int32, (nrow_q, r), 1)
        tok = lax.broadcasted_iota(jnp.int32, (nrow_q, r), 0) // MLA_HEADS
        back(front(cnew_ref[...], krnew_ref[...], cosn_ref[...], sinn_ref[...]), (key < ts) & (key <= tok))
        o_ref[...] = acc_ref[...] / l_ref[...]


def _sample_attention(layer, cache_latent, cache_k_rope, page_table, cnew, krnew, tabs, wuk_t, qabs, qrope, kgr2,
                      pps, sub):
    bs, n_pages = page_table.shape
    page = cache_latent.shape[2]
    nch = n_pages // pps
    ts = qabs.shape[1] // MLA_HEADS
    cosb, sinb, loc, cosn, sinn = tabs
    pt_flat = page_table.reshape(-1)
    const = lambda a: pl.BlockSpec(a.shape, lambda b, j, pt: (0,) * a.ndim)
    per_seq = lambda a: pl.BlockSpec((None,) + a.shape[1:], lambda b, j, pt: (b,) + (0,) * (a.ndim - 1))
    hbm = pl.BlockSpec(memory_space=pl.ANY)
    in_specs = [hbm, hbm, per_seq(cnew), per_seq(krnew), const(cosb), const(sinb), const(loc),
                const(cosn), const(sinn), const(wuk_t), per_seq(qabs), per_seq(qrope), const(kgr2)]
    nrow_q = qabs.shape[1]
    grid_spec = pltpu.PrefetchScalarGridSpec(
        num_scalar_prefetch=1,
        grid=(bs, nch + 1),
        in_specs=in_specs,
        out_specs=pl.BlockSpec((None, nrow_q, KV_RANK), lambda b, j, pt: (b, 0, 0)),
        scratch_shapes=[pltpu.VMEM((MLA_HEADS * MLA_NOPE + nrow_q, KV_RANK), BF16),
                        pltpu.VMEM((nrow_q, 1), F32), pltpu.VMEM((nrow_q, 1), F32),
                        pltpu.VMEM((nrow_q, KV_RANK), F32),
                        pltpu.VMEM((2, pps * page, KV_RANK), F32),
                        pltpu.VMEM((2, MLA_ROPE, pps * page), F32),
                        pltpu.SemaphoreType.DMA((2,)), pltpu.SemaphoreType.DMA((2,))])
    return pl.pallas_call(
        functools.partial(_sattn_body, layer=layer, pps=pps, sub=sub, nch=nch, ts=ts, n_pages=n_pages, page=page),
        grid_spec=grid_spec,
        out_shape=jax.ShapeDtypeStruct((bs, nrow_q, KV_RANK), F32),
        compiler_params=_cparams(("arbitrary", "arbitrary")),
        name="sample_attention",
    )(pt_flat, cache_latent, cache_k_rope, cnew, krnew, cosb, sinb, loc, cosn, sinn, wuk_t, qabs, qrope, kgr2)


def _uvproj_body(o_ref, w_ref, out_ref):
    out_ref[...] = _dot(o_ref[...].astype(BF16), w_ref[...])


def _uv_project(olat, wuv):
    ns = olat.shape[0]
    return pl.pallas_call(
        _uvproj_body,
        grid=(MLA_HEADS,),
        in_specs=[pl.BlockSpec((ns, KV_RANK), lambda h: (0, h)),
                  pl.BlockSpec((KV_RANK, MLA_V), lambda h: (0, h))],
        out_specs=pl.BlockSpec((ns, MLA_V), lambda h: (0, h)),
        out_shape=jax.ShapeDtypeStruct((ns, MLA_HEADS * MLA_V), F32),
        compiler_params=_cparams(("arbitrary",)),
        name="uv_project",
    )(olat, wuv)


CONV_PAD = 32


def _conv_body(a_ref, g_ref, st_ref, w_ref, b_ref, lng_ref, lnb_ref, o_ref, st_out_ref, ext, *, nseq, t, rc):
    ctx = CONV_WIDTH - 1
    u = a_ref[...] * jax.nn.sigmoid(g_ref[...])
    for s in range(nseq):
        ext[s, 0:SUBLANES, :] = jnp.zeros((SUBLANES, CONV_CH), F32)
        ext[s, CONV_PAD - ctx:CONV_PAD, :] = st_ref[s]
        ext[s, CONV_PAD:CONV_PAD + t, :] = u[s * t:(s + 1) * t, :]
        ext[s, CONV_PAD + t:, :] = jnp.zeros((ext.shape[1] - CONV_PAD - t, CONV_CH), F32)
    bias = b_ref[...]
    lng = lng_ref[...]
    lnb = lnb_ref[...]

    def chunk(s, r0):
        span = ((rc + SUBLANES - 1) // SUBLANES) * SUBLANES
        win = ext[s, pl.ds(r0, span + CONV_PAD + SUBLANES), :]
        acc = jnp.zeros((rc, CONV_CH), F32) + bias
        for b in range(SUBLANES):
            off = CONV_PAD - ctx + b
            sh = win[off:off + span + CONV_PAD - SUBLANES, :]
            for a in range((CONV_WIDTH - b + SUBLANES - 1) // SUBLANES):
                w = SUBLANES * a + b
                acc = acc + sh[SUBLANES * a:SUBLANES * a + rc, :] * w_ref[w:w + 1, :]
        mu = jnp.mean(acc, axis=-1, keepdims=True)
        cen = acc - mu
        var = jnp.mean(cen * cen, axis=-1, keepdims=True)
        y = cen * lax.rsqrt(var + EPS) * lng + lnb
        o_ref[pl.ds(s * t + r0, rc), :] = y * jax.nn.sigmoid(y)

    for s in range(nseq):
        if t // rc == 1:
            chunk(s, 0)
        else:
            def body(i, carry, s=s):
                chunk(s, pl.multiple_of(i * rc, rc))
                return carry
            lax.fori_loop(0, t // rc, body, 0)
        st_out_ref[s] = ext[s, t + CONV_PAD - ctx:t + CONV_PAD, :]


def _conv(z, row0, state, nseq, t, cw, cb, lng, lnb):
    bsz = state.shape[0]
    rows = nseq * t
    r0 = row0 // rows
    rc = min(t, 32)
    const = lambda a: pl.BlockSpec(a.shape, lambda i: (0,) * a.ndim)
    return pl.pallas_call(
        functools.partial(_conv_body, nseq=nseq, t=t, rc=rc),
        grid=(bsz // nseq,),
        in_specs=[pl.BlockSpec((rows, CONV_CH), lambda i: (r0 + i, Z_CA // CONV_CH)),
                  pl.BlockSpec((rows, CONV_CH), lambda i: (r0 + i, Z_CG // CONV_CH)),
                  pl.BlockSpec((nseq, CONV_WIDTH - 1, CONV_CH), lambda i: (i, 0, 0)),
                  const(cw), const(cb), const(lng), const(lnb)],
        out_specs=(pl.BlockSpec((rows, CONV_CH), lambda i: (i, 0)),
                   pl.BlockSpec((nseq, CONV_WIDTH - 1, CONV_CH), lambda i: (i, 0, 0))),
        out_shape=(jax.ShapeDtypeStruct((bsz * t, CONV_CH), F32),
                   jax.ShapeDtypeStruct((bsz, CONV_WIDTH - 1, CONV_CH), F32)),
        scratch_shapes=[pltpu.VMEM((nseq, CONV_PAD + max(t, SUBLANES) + SUBLANES, CONV_CH), F32)],
        compiler_params=_cparams(("arbitrary",)),
        name="conv_module_t%d" % t,
    )(z, z, state, cw, cb, lng, lnb)


def _gla_matrices(rows, seq_len):
    t = np.arange(rows)[:, None]
    u = np.arange(rows)[None, :]
    same = (t // seq_len) == (u // seq_len)
    blocks = [same & (u <= t), same & (u > t)]
    nlev = int(round(math.log2(seq_len)))
    for j in range(nlev):
        h = 2 ** j
        mid = (t // (2 * h)) * (2 * h) + h - 1
        second = (t % (2 * h)) >= h
        blocks.append(np.where(second, (u > mid) & (u <= t), (u > t) & (u <= mid)))
    return np.concatenate(blocks, axis=0).astype(np.float32), nlev


def _gla_body(gq_ref, gk_ref, gv_ref, krl_ref, go_ref, s0_ref, gwp_ref, gb_ref, ng_ref, mall_ref,
              o_ref, sout_ref, st, qd_s, kd_s, ghi_s, glo_s, oi_s, *, seq_len, nlev):
    ci = pl.program_id(1)
    rows = gq_ref.shape[0]
    nh, dk, dv = GLA_HEADS, GLA_DK, GLA_DV
    rg = max(seq_len, SUBLANES)
    spg = rg // seq_len

    @pl.when(ci == 0)
    def _():
        st[...] = s0_ref[...]

    a_hi, a_lo = _split_hi_lo(krl_ref[...])
    w_hi, w_lo = _split_hi_lo(gwp_ref[...])
    x = _dot(a_hi, w_hi) + _dot(a_lo, w_hi) + _dot(a_hi, w_lo) + gb_ref[...]
    g = jax.nn.log_sigmoid(x) * (1.0 / GLA_TAU)
    g_hi, g_lo = _split_hi_lo(g)
    mall = mall_ref[...]
    d = _dot(mall, g_hi) + _dot(mall, g_lo)
    q = gq_ref[...] * (dk ** -0.5)
    k = gk_ref[...]
    v = gv_ref[...].astype(BF16)
    lane_head = lax.broadcasted_iota(jnp.int32, (rows, nh * dk), 1) // dk
    tcol = lax.broadcasted_iota(jnp.int32, (rows, 1), 0)
    rowi = lax.broadcasted_iota(jnp.int32, (rows, rows), 0)
    coli = lax.broadcasted_iota(jnp.int32, (rows, rows), 1)

    def head_stack(a):
        return jnp.concatenate([jnp.where(lane_head == h, a, 0.0) for h in range(nh)], axis=0).astype(BF16)

    p = _dot_nt(head_stack(q), k.astype(BF16))
    amat = [jnp.where(rowi == coli, p[h * rows:(h + 1) * rows], 0.0) for h in range(nh)]
    for j in range(nlev):
        half = 2 ** j
        e = jnp.exp(d[(2 + j) * rows:(3 + j) * rows])
        second = (tcol % (2 * half)) >= half
        qj = jnp.where(second, q * e, 0.0)
        kj = jnp.where(second, 0.0, k * e)
        p = _dot_nt(head_stack(qj), kj.astype(BF16))
        same = (rowi // (2 * half)) == (coli // (2 * half))
        amat = [amat[h] + jnp.where(same, p[h * rows:(h + 1) * rows], 0.0) for h in range(nh)]
    for h in range(nh):
        oi_s[:, h * dv:(h + 1) * dv] = _dot(amat[h].astype(BF16), v[:, h * dv:(h + 1) * dv])

    qd_s[...] = q * jnp.exp(d[0:rows])
    kd_s[...] = k * jnp.exp(d[rows:2 * rows])
    ghi_s[...] = g_hi.astype(F32)
    glo_s[...] = g_lo.astype(F32)

    ones = jnp.ones((rg, dv), BF16)
    gl_head = lax.broadcasted_iota(jnp.int32, (rg, nh * dk), 1) // dk
    g_row = lax.broadcasted_iota(jnp.int32, (rg, 1), 0) // seq_len
    st_head = lax.broadcasted_iota(jnp.int32, (nh * dk, dv), 0) // dk

    def group(gi, carry):
        r0 = pl.multiple_of(gi * rg, rg)
        qd = qd_s[pl.ds(r0, rg), :]
        kd = kd_s[pl.ds(r0, rg), :]
        ghi = ghi_s[pl.ds(r0, rg), :].astype(BF16)
        glo = glo_s[pl.ds(r0, rg), :].astype(BF16)
        vg = gv_ref[pl.ds(r0, rg), :].astype(BF16)
        o_inter = [jnp.zeros((rg, dv), F32) for _ in range(nh)]
        for w in range(spg):
            sidx = gi * spg + w
            s_old = st[sidx]
            s_bf = s_old.astype(BF16)
            mine = g_row == w
            qw = jnp.where(mine, qd, 0.0)
            kw = jnp.where(mine, kd, 0.0).astype(BF16)
            bl = (_dot_tn(jnp.where(mine, ghi, jnp.zeros_like(ghi)), ones)
                  + _dot_tn(jnp.where(mine, glo, jnp.zeros_like(glo)), ones))
            upd = jnp.zeros((nh * dk, dv), F32)
            for h in range(nh):
                o_inter[h] = o_inter[h] + _dot(jnp.where(gl_head == h, qw, 0.0).astype(BF16), s_bf)
                u_h = _dot_tn(kw, vg[:, h * dv:(h + 1) * dv])
                upd = upd + jnp.where(st_head == h, u_h, 0.0)
            st[sidx] = jnp.exp(bl) * s_old + upd
        for h in range(nh):
            oi_s[pl.ds(r0, rg), h * dv:(h + 1) * dv] = oi_s[pl.ds(r0, rg), h * dv:(h + 1) * dv] + o_inter[h]
        return carry

    ngroups = rows // rg
    if ngroups == 1:
        group(0, 0)
    else:
        lax.fori_loop(0, ngroups, group, 0)

    go = go_ref[...]
    for h in range(nh):
        oh = oi_s[:, h * dv:(h + 1) * dv]
        on = oh * lax.rsqrt(jnp.mean(oh * oh, axis=-1, keepdims=True) + EPS) * ng_ref[...]
        gh = go[:, h * dv:(h + 1) * dv]
        o_ref[:, h * dv:(h + 1) * dv] = on * (gh * jax.nn.sigmoid(gh))

    @pl.when(ci == pl.num_programs(1) - 1)
    def _():
        sout_ref[...] = st[...]


def _gla(z, row0, s0, rows, seq_len, nchunks, gwp, gb, ng):
    bsz = s0.shape[0]
    nseq = rows // seq_len if nchunks == 1 else 1
    ngrid = bsz // nseq
    mall_np, nlev = _gla_matrices(rows, seq_len)
    mall = jnp.asarray(mall_np, BF16)
    r0 = row0 // rows
    nk = GLA_HEADS * GLA_DK
    nv = GLA_HEADS * GLA_DV
    zspec = lambda width, col: pl.BlockSpec((rows, width), lambda b, c: (r0 + b * nchunks + c, col // width))
    const = lambda a: pl.BlockSpec(a.shape, lambda b, c: (0,) * a.ndim)
    sspec = pl.BlockSpec((nseq, nk, GLA_DV), lambda b, c: (b, 0, 0))
    return pl.pallas_call(
        functools.partial(_gla_body, seq_len=seq_len, nlev=nlev),
        grid=(ngrid, nchunks),
        in_specs=[zspec(nk, Z_GQ), zspec(nk, Z_GK), zspec(nv, Z_GV), zspec(LANES, Z_KRL), zspec(nv, Z_GO),
                  sspec, const(gwp), const(gb), const(ng), const(mall)],
        out_specs=(pl.BlockSpec((rows, nv), lambda b, c: (b * nchunks + c, 0)), sspec),
        out_shape=(jax.ShapeDtypeStruct((ngrid * nchunks * rows, nv), F32),
                   jax.ShapeDtypeStruct((bsz, nk, GLA_DV), F32)),
        scratch_shapes=[pltpu.VMEM((nseq, nk, GLA_DV), F32),
                        pltpu.VMEM((rows, nk), F32), pltpu.VMEM((rows, nk), F32),
                        pltpu.VMEM((rows, nk), F32), pltpu.VMEM((rows, nk), F32),
                        pltpu.VMEM((rows, nv), F32)],
        compiler_params=_cparams(("arbitrary", "arbitrary")),
        name="gla_l%d" % seq_len,
    )(z, z, z, z, z, s0, gwp, gb, ng, mall)


def _outproj_body(x_ref, mp_ref, cp_ref, gp_ref, ms_ref, cs_ref, gs_ref, w_ref, o_ref, *, np_tiles):
    is_sample = pl.program_id(1) >= np_tiles
    pick = lambda p_ref, s_ref: jnp.where(is_sample, s_ref[...], p_ref[...])
    mix = jnp.concatenate([pick(mp_ref, ms_ref), pick(cp_ref, cs_ref), pick(gp_ref, gs_ref)], axis=1).astype(BF16)
    o_ref[...] = x_ref[...] + _dot(mix, w_ref[...])


def _outproj(x, mixers_p, mixers_s, w, tm):
    n, d = x.shape
    tn = d // 2
    np_tiles = mixers_p[0].shape[0] // tm
    pspec = lambda a: pl.BlockSpec((tm, a.shape[1]), lambda j, i: (jnp.minimum(i, np_tiles - 1), 0))
    sspec = lambda a: pl.BlockSpec((tm, a.shape[1]), lambda j, i: (jnp.maximum(i - np_tiles, 0), 0))
    return pl.pallas_call(
        functools.partial(_outproj_body, np_tiles=np_tiles),
        grid=(2, n // tm),
        in_specs=([pl.BlockSpec((tm, tn), lambda j, i: (i, j))]
                  + [pspec(a) for a in mixers_p] + [sspec(a) for a in mixers_s]
                  + [pl.BlockSpec((w.shape[0], tn), lambda j, i: (0, j))]),
        out_specs=pl.BlockSpec((tm, tn), lambda j, i: (i, j)),
        out_shape=jax.ShapeDtypeStruct((n, d), F32),
        compiler_params=_cparams(("arbitrary", "arbitrary")),
        name="outproj",
    )(x, *mixers_p, *mixers_s, w)


def _router_body(x_ref, g_ref, w_ref, b_ref, h_ref, e_ref, gate_ref):
    x = x_ref[...]
    h = (x * lax.rsqrt(jnp.mean(x * x, axis=-1, keepdims=True) + EPS)) * g_ref[...]
    h_ref[...] = h
    h_hi, h_lo = _split_hi_lo(h)
    w_hi, w_lo = _split_hi_lo(w_ref[...])
    logits = _dot(h_hi, w_hi) + _dot(h_lo, w_hi) + _dot(h_hi, w_lo) + b_ref[...]
    lane = lax.broadcasted_iota(jnp.int32, logits.shape, 1)
    far = jnp.int32(4 * LANES)
    red_max = lambda a: jnp.max(a, axis=-1, keepdims=True)
    red_min = lambda a: jnp.min(a, axis=-1, keepdims=True)
    red_sum = lambda a: jnp.sum(a, axis=-1, keepdims=True)

    is_g = lane < N_GROUPS
    eg = jnp.where(is_g, jnp.exp(logits - red_max(jnp.where(is_g, logits, NEG_BIG))), 0.0)
    pg = eg / red_sum(eg)
    g_prob = red_max(pg)
    g_idx = red_min(jnp.where(is_g & (pg == g_prob), lane, far))
    lo = N_GROUPS + g_idx * EXPERTS_PER_GROUP
    in_g = (lane >= lo) & (lane < lo + EXPERTS_PER_GROUP)
    ee = jnp.where(in_g, jnp.exp(logits - red_max(jnp.where(in_g, logits, NEG_BIG))), 0.0)
    pe = ee / red_sum(ee)
    p1 = red_max(pe)
    i1 = red_min(jnp.where(in_g & (pe == p1), lane, far))
    rest = in_g & (lane != i1)
    p2 = red_max(jnp.where(rest, pe, -1.0))
    i2 = red_min(jnp.where(rest & (pe == p2), lane, far))
    den = p1 + p2
    e_ref[...] = jnp.where(lane == 0, i1 - N_GROUPS, jnp.where(lane == 1, i2 - N_GROUPS, 0))
    gate_ref[...] = jnp.where(lane == 0, g_prob * p1 / den, jnp.where(lane == 1, g_prob * p2 / den, 0.0))


def _router(x, gain, w, b, tm):
    n, d = x.shape
    const = lambda a: pl.BlockSpec(a.shape, lambda i: (0,) * a.ndim)
    return pl.pallas_call(
        _router_body,
        grid=(n // tm,),
        in_specs=[pl.BlockSpec((tm, d), lambda i: (i, 0)), const(gain), const(w), const(b)],
        out_specs=(pl.BlockSpec((tm, d), lambda i: (i, 0)),
                   pl.BlockSpec((tm, LANES), lambda i: (i, 0)),
                   pl.BlockSpec((tm, LANES), lambda i: (i, 0))),
        out_shape=(jax.ShapeDtypeStruct((n, d), F32),
                   jax.ShapeDtypeStruct((n, LANES), jnp.int32),
                   jax.ShapeDtypeStruct((n, LANES), F32)),
        compiler_params=_cparams(("arbitrary",)),
        name="moe_router",
    )(x, gain, w, b)


def _ffn_body(bexp_ref, rtok_ref, rout_ref, h_hbm, gate_ref, wg_ref, wu_ref, wd_ref, oa_hbm,
              xbuf, obuf, wgb, wub, wdb, gsem, ssem, *, bm):
    i = pl.program_id(0)
    nb = pl.num_programs(0)
    slot = i % 2
    nxt = jnp.minimum(i + 1, nb - 1)
    prev = jnp.where(i > 0, i - 1, nb)

    def gather_start(step, sl):
        for r in range(bm):
            pltpu.make_async_copy(h_hbm.at[rtok_ref[step * bm + r]], xbuf.at[sl, r], gsem.at[sl]).start()

    def gather_wait(sl):
        for r in range(bm):
            pltpu.make_async_copy(h_hbm.at[0], xbuf.at[sl, r], gsem.at[sl]).wait()

    def scatter_start(step, sl):
        for r in range(bm):
            pltpu.make_async_copy(obuf.at[sl, r], oa_hbm.at[rout_ref[step * bm + r]], ssem).start()

    def scatter_wait():
        for r in range(bm):
            pltpu.make_async_copy(obuf.at[0, r], oa_hbm.at[0], ssem).wait()

    @pl.when(i == 0)
    def _():
        gather_start(0, 0)
        obuf[...] = jnp.zeros(obuf.shape, F32)
        scatter_start(nb + 1, 0)

    last = bexp_ref[jnp.maximum(i - 1, 0)]
    @pl.when((i == 0) | (bexp_ref[i] != last))
    def _():
        wgb[...] = wg_ref[...].astype(BF16)
        wub[...] = wu_ref[...].astype(BF16)
        wdb[...] = wd_ref[...].astype(BF16)

    gather_wait(slot)
    scatter_wait()
    x = xbuf[slot].astype(BF16)
    gather_start(nxt, 1 - slot)
    scatter_start(prev, 1 - slot)
    a = _dot_nt(x, wgb[...])
    u = _dot_nt(x, wub[...])
    mid = (a * jax.nn.sigmoid(a) * u).astype(BF16)
    obuf[slot] = _dot(mid, wdb[...]) * gate_ref[...]

    @pl.when(i == nb - 1)
    def _():
        scatter_start(i, slot)
        gather_wait(1 - slot)
        scatter_wait()
        scatter_wait()


def _expert_ffn(layer, h, block_exp, row_tok, row_out, row_gate, wg_t, wu_t, w_down, bm):
    n, d = h.shape
    nb = block_exp.shape[0]
    de = w_down.shape[2]
    wspec = pl.BlockSpec((None, None, de, d), lambda i, be, rt, ro: (layer, be[i], 0, 0))
    grid_spec = pltpu.PrefetchScalarGridSpec(
        num_scalar_prefetch=3,
        grid=(nb,),
        in_specs=[pl.BlockSpec(memory_space=pl.ANY),
                  pl.BlockSpec((bm, 1), lambda i, be, rt, ro: (i, 0)),
                  wspec, wspec, wspec],
        out_specs=pl.BlockSpec(memory_space=pl.ANY),
        scratch_shapes=[pltpu.VMEM((2, bm, d), F32), pltpu.VMEM((2, bm, d), F32),
                        pltpu.VMEM((de, d), BF16), pltpu.VMEM((de, d), BF16), pltpu.VMEM((de, d), BF16),
                        pltpu.SemaphoreType.DMA((2,)), pltpu.SemaphoreType.DMA(())])
    return pl.pallas_call(
        functools.partial(_ffn_body, bm=bm),
        grid_spec=grid_spec,
        out_shape=jax.ShapeDtypeStruct(((nb + 2) * bm, d), F32),
        compiler_params=pltpu.CompilerParams(dimension_semantics=("arbitrary",),
                                             vmem_limit_bytes=FFN_VMEM_LIMIT_BYTES),
        name="expert_ffn",
    )(block_exp, row_tok, row_out, h, row_gate, wg_t, wu_t, w_down)


def _combine_body(x_ref, a_ref, b_ref, o_ref):
    o_ref[...] = x_ref[...] + (a_ref[...] + b_ref[...])


def _combine(x, oa, tm):
    n, d = x.shape
    nt = n // tm
    return pl.pallas_call(
        _combine_body,
        grid=(nt,),
        in_specs=[pl.BlockSpec((tm, d), lambda i: (i, 0)), pl.BlockSpec((tm, d), lambda i: (i, 0)),
                  pl.BlockSpec((tm, d), lambda i: (nt + i, 0))],
        out_specs=pl.BlockSpec((tm, d), lambda i: (i, 0)),
        out_shape=jax.ShapeDtypeStruct((n, d), F32),
        compiler_params=_cparams(("arbitrary",)),
        name="moe_combine",
    )(x, oa, oa)


def _dispatch_plan(experts, gates, bm):
    n = experts.shape[0]
    a = n * TOP_K
    e_flat = experts.reshape(a)
    g_flat = gates.reshape(a)
    onehot = (e_flat[:, None] == jnp.arange(N_EXPERTS, dtype=jnp.int32)[None, :]).astype(jnp.int32)
    csum = jnp.cumsum(onehot, axis=0)
    counts = csum[-1]
    rank = jnp.take_along_axis(csum, e_flat[:, None], axis=1)[:, 0] - 1
    padded = (counts + bm - 1) // bm * bm
    pad_end = jnp.cumsum(padded)
    pad_start = pad_end - padded
    dest = pad_start[e_flat] + rank
    nb = -(-(a + N_EXPERTS * (bm - 1)) // bm)
    rows = nb * bm
    row_asg = jnp.full((rows,), -1, jnp.int32).at[dest].set(jnp.arange(a, dtype=jnp.int32))
    valid = row_asg >= 0
    safe = jnp.maximum(row_asg, 0)
    row_tok = jnp.where(valid, safe // TOP_K, 0).astype(jnp.int32)
    row_gate = jnp.where(valid, g_flat[safe], 0.0).astype(F32)[:, None]
    spare = a + jnp.cumsum(jnp.logical_not(valid).astype(jnp.int32)) - 1
    row_out = jnp.where(valid, (safe % TOP_K) * n + safe // TOP_K, spare).astype(jnp.int32)
    row_out = jnp.concatenate([row_out, rows + jnp.arange(2 * bm, dtype=jnp.int32)])
    starts = jnp.arange(nb, dtype=jnp.int32) * bm
    block_exp = jnp.minimum(jnp.sum((pad_end[None, :] <= starts[:, None]).astype(jnp.int32), axis=1),
                            N_EXPERTS - 1).astype(jnp.int32)
    return block_exp, row_tok, row_out, row_gate


FFN_BLOCK_ROWS = 256


def kernel(x_prompt, x_sample, cache_latent, cache_k_rope, state_conv, state_gla, page_table, ln_mix, w_in,
           latent_gain, q_gain, k_gain, w_uk, w_uv, conv_w, conv_b, conv_ln_g, conv_ln_b, gla_gate_w, gla_gate_b,
           gla_norm_g, w_out, ln_ffn, router_group_w, router_group_b, router_expert_w, router_expert_b,
           w_gate, w_up, w_down):
    bp, sp, d = x_prompt.shape
    bs, ts, _ = x_sample.shape
    depth = w_in.shape[0]
    n_p, n_s = bp * sp, bs * ts
    n = n_p + n_s
    n_pages = page_table.shape[1]
    page = cache_latent.shape[2]
    past = n_pages * page
    tm = math.gcd(math.gcd(n_p, n_s), 512)
    tmb = math.gcd(tm, 256)
    tq = math.gcd(sp, 256)
    pps = min(SATTN_PPS, n_pages)
    sub = min(SATTN_SUB, pps)
    assert n_pages % pps == 0 and pps % sub == 0 and sp % tmb == 0 and tmb % ts == 0
    gla_rows = min(128, sp)
    gla_rows_s = min(128, n_s)
    assert sp % gla_rows == 0 and n_s % gla_rows_s == 0 and gla_rows_s % max(ts, SUBLANES) == 0
    assert n_p % gla_rows_s == 0 and (ts & (ts - 1)) == 0
    conv_nseq = SUBLANES
    assert bs % conv_nseq == 0 and n_p % (conv_nseq * ts) == 0

    cos_p, sin_p = _rope_tables(np.arange(sp))
    cos_s, sin_s = _rope_tables(past + (np.arange(tmb) % ts))
    tabs = _sample_rope_tables(past, page, sub * page)
    cos_p, sin_p, cos_s, sin_s = (jnp.asarray(t) for t in (cos_p, sin_p, cos_s, sin_s))
    kr_cache_t = jnp.swapaxes(cache_k_rope, 2, 3)
    wg_t = jnp.swapaxes(w_gate, 2, 3)
    wu_t = jnp.swapaxes(w_up, 2, 3)
    w_in_t = jnp.swapaxes(w_in, 1, 2)

    src = _z_source_columns()
    src_idx = jnp.asarray(np.maximum(src, 0), jnp.int32)
    src_ok = jnp.asarray(src >= 0)

    x = jnp.concatenate([x_prompt.reshape(n_p, d), x_sample.reshape(n_s, d)], axis=0)
    zeros_conv = jnp.zeros((bp, CONV_WIDTH - 1, CONV_CH), F32)
    zeros_gla = jnp.zeros((bp, GLA_HEADS * GLA_DK, GLA_DV), F32)
    outs = [[] for _ in range(8)]

    for l in range(depth):
        row2 = lambda a: a.reshape(1, -1).astype(F32)
        w_in_p = jnp.where(src_ok[:, None], jnp.take(w_in_t[l], src_idx, axis=0), 0.0).astype(BF16)
        wuk = w_uk[l].reshape(KV_RANK, MLA_HEADS * MLA_NOPE).astype(BF16)
        wuv = w_uv[l].reshape(KV_RANK, MLA_HEADS * MLA_V).astype(BF16)
        qgn, kgn = row2(q_gain[l, :MLA_NOPE]), row2(k_gain[l, :MLA_NOPE])
        qgr2 = row2(jnp.tile(q_gain[l, MLA_NOPE:], 2))
        kgr2 = row2(jnp.tile(k_gain[l, MLA_NOPE:], 2))
        lg = row2(latent_gain[l])

        z = _inproj(x, row2(ln_mix[l]), w_in_p, tm)

        c_p, q_p, k_p, v_p = _prep(z, 0, n_p, tmb, cos_p, sin_p, False, lg, qgn, qgr2, kgn, kgr2, wuk, wuv)
        c_s, qrope_s, qabs_s = _prep(z, n_p, n_s, tmb, cos_s, sin_s, True, lg, qgn, qgr2, kgn, kgr2, wuk, wuv)
        mla_p = _prompt_attention(q_p, k_p, v_p, bp, sp, tq)
        kr_all = z[:, Z_KRL:Z_KRL + MLA_ROPE]
        cnew = jnp.pad(c_s.reshape(bs, ts, KV_RANK), ((0, 0), (0, page - ts), (0, 0)))
        krnew = jnp.pad(jnp.swapaxes(kr_all[n_p:].reshape(bs, ts, MLA_ROPE), 1, 2),
                        ((0, 0), (0, 0), (0, page - ts)))
        kg_lanes = jnp.broadcast_to(k_gain[l, MLA_NOPE:].astype(F32)[:, None], (MLA_ROPE, LANES))
        o_lat = _sample_attention(l, cache_latent, kr_cache_t, page_table, cnew, krnew, tabs, wuk.T,
                                  qabs_s.reshape(bs, ts * MLA_HEADS, KV_RANK),
                                  qrope_s.reshape(bs, ts * MLA_HEADS, MLA_ROPE), kg_lanes, pps, sub)
        mla_s = _uv_project(o_lat.reshape(n_s, MLA_HEADS * KV_RANK), wuv)

        cw, cb = conv_w[l].astype(F32), row2(conv_b[l])
        clg, clb = row2(conv_ln_g[l]), row2(conv_ln_b[l])
        conv_p, cst_p = _conv(z, 0, zeros_conv, 1, sp, cw, cb, clg, clb)
        conv_s, cst_s = _conv(z, n_p, state_conv[l], conv_nseq, ts, cw, cb, clg, clb)

        gwp = jnp.zeros((LANES, GLA_HEADS * GLA_DK), F32).at[MLA_ROPE:MLA_ROPE + GLA_GATE_RANK].set(gla_gate_w[l])
        ggb, gng = row2(gla_gate_b[l]), row2(gla_norm_g[l])
        gla_p, gst_p = _gla(z, 0, zeros_gla, gla_rows, gla_rows, sp // gla_rows, gwp, ggb, gng)
        gla_s, gst_s = _gla(z, n_p, state_gla[l].reshape(bs, GLA_HEADS * GLA_DK, GLA_DV), gla_rows_s, ts, 1,
                            gwp, ggb, gng)

        x = _outproj(x, (mla_p, conv_p, gla_p), (mla_s, conv_s, gla_s), w_out[l].astype(BF16), tm)

        wr = jnp.zeros((d, LANES), F32)
        wr = wr.at[:, :N_GROUPS].set(router_group_w[l]).at[:, N_GROUPS:N_GROUPS + N_EXPERTS].set(router_expert_w[l])
        br = jnp.zeros((1, LANES), F32)
        br = br.at[0, :N_GROUPS].set(router_group_b[l]).at[0, N_GROUPS:N_GROUPS + N_EXPERTS].set(router_expert_b[l])
        h, e_idx, gate = _router(x, row2(ln_ffn[l]), wr, br, tm)
        plan = _dispatch_plan(e_idx[:, :TOP_K], gate[:, :TOP_K], FFN_BLOCK_ROWS)
        oa = _expert_ffn(l, h, *plan, wg_t, wu_t, w_down, FFN_BLOCK_ROWS)
        x = _combine(x, oa, tm)

        outs[0].append(c_p.reshape(bp, sp, KV_RANK))
        outs[1].append(kr_all[:n_p].reshape(bp, sp, MLA_ROPE))
        outs[2].append(cst_p)
        outs[3].append(gst_p.reshape(bp, GLA_HEADS, GLA_DK, GLA_DV))
        outs[4].append(c_s.reshape(bs, ts, KV_RANK))
        outs[5].append(kr_all[n_p:].reshape(bs, ts, MLA_ROPE))
        outs[6].append(cst_s)
        outs[7].append(gst_s.reshape(bs, GLA_HEADS, GLA_DK, GLA_DV))

    return (x[:n_p].reshape(bp, sp, d), x[n_p:].reshape(bs, ts, d)) + tuple(jnp.stack(o) for o in outs)
```

```python
import functools
import math

import numpy as np
import jax
import jax.numpy as jnp
from jax import lax
from jax.experimental import pallas as pl
from jax.experimental.pallas import tpu as pltpu

F32 = jnp.float32
BF16 = jnp.bfloat16

MLA_HEADS = 8
MLA_NOPE = 128
MLA_ROPE = 64
MLA_V = 128
MLA_QK = MLA_NOPE + MLA_ROPE
KV_RANK = 512
ROPE_BASE = 10000.0
CONV_CH = 512
CONV_WIDTH = 31
GLA_HEADS = 4
GLA_DV = 128
GLA_DK = 64
GLA_GATE_RANK = 16
GLA_TAU = 16.0
N_GROUPS = 4
EXPERTS_PER_GROUP = 8
N_EXPERTS = N_GROUPS * EXPERTS_PER_GROUP
TOP_K = 2
D_EXPERT = 704
EPS = 1e-6
NEG_BIG = -1e30

LANES = 128
SUBLANES = 8
VMEM_LIMIT_BYTES = 56 * 1024 * 1024
FFN_VMEM_LIMIT_BYTES = 60 * 1024 * 1024

Z_QN = 0
Z_QR = 1024
Z_C = 1536
Z_CA = 2048
Z_CG = 2560
Z_GQ = 3072
Z_GK = 3328
Z_GV = 3584
Z_GO = 4096
Z_KRL = 4608
Z_COLS = 4864


def _z_source_columns():
    src = -np.ones((Z_COLS,), np.int64)
    q0 = 0
    for h in range(MLA_HEADS):
        src[Z_QN + h * MLA_NOPE:Z_QN + (h + 1) * MLA_NOPE] = q0 + h * MLA_QK + np.arange(MLA_NOPE)
        src[Z_QR + h * MLA_ROPE:Z_QR + (h + 1) * MLA_ROPE] = q0 + h * MLA_QK + MLA_NOPE + np.arange(MLA_ROPE)
    o = MLA_HEADS * MLA_QK
    src[Z_C:Z_C + KV_RANK] = o + np.arange(KV_RANK); o += KV_RANK
    src[Z_KRL:Z_KRL + MLA_ROPE] = o + np.arange(MLA_ROPE); o += MLA_ROPE
    src[Z_CA:Z_CA + CONV_CH] = o + np.arange(CONV_CH); o += CONV_CH
    src[Z_CG:Z_CG + CONV_CH] = o + np.arange(CONV_CH); o += CONV_CH
    n = GLA_HEADS * GLA_DK
    src[Z_GQ:Z_GQ + n] = o + np.arange(n); o += n
    src[Z_GK:Z_GK + n] = o + np.arange(n); o += n
    n = GLA_HEADS * GLA_DV
    src[Z_GV:Z_GV + n] = o + np.arange(n); o += n
    src[Z_KRL + MLA_ROPE:Z_KRL + MLA_ROPE + GLA_GATE_RANK] = o + np.arange(GLA_GATE_RANK); o += GLA_GATE_RANK
    src[Z_GO:Z_GO + n] = o + np.arange(n); o += n
    return src


def _cparams(sem):
    return pltpu.CompilerParams(dimension_semantics=sem, vmem_limit_bytes=VMEM_LIMIT_BYTES)


def _dot(a, b):
    return jnp.dot(a, b, preferred_element_type=F32)


def _dot_nt(a, b):
    return lax.dot_general(a, b, (((1,), (1,)), ((), ())), preferred_element_type=F32)


def _dot_tn(a, b):
    return lax.dot_general(a, b, (((0,), (0,)), ((), ())), preferred_element_type=F32)


def _split_hi_lo(a):
    hi = a.astype(BF16)
    lo = (a - hi.astype(F32)).astype(BF16)
    return hi, lo


def _rope_tables(pos, period_lanes=LANES):
    half = MLA_ROPE // 2
    inv = ROPE_BASE ** (-np.arange(half, dtype=np.float64) / half)
    lane = np.arange(period_lanes)
    ang = np.asarray(pos, np.float64)[:, None] * inv[lane % half][None, :]
    sign = np.where((lane % MLA_ROPE) < half, -1.0, 1.0)[None, :]
    return np.cos(ang).astype(np.float32), (np.sin(ang) * sign).astype(np.float32)


def _inproj_body(x_ref, g_ref, w_ref, o_ref):
    x = x_ref[...]
    ms = jnp.mean(x * x, axis=-1, keepdims=True)
    xn = (x * lax.rsqrt(ms + EPS)) * g_ref[...]
    o_ref[...] = _dot_nt(xn.astype(BF16), w_ref[...])


def _inproj(x, gain, w, tm):
    n, d = x.shape
    zc = w.shape[0]
    tn = zc // 2
    return pl.pallas_call(
        _inproj_body,
        grid=(2, n // tm),
        in_specs=[pl.BlockSpec((tm, d), lambda j, i: (i, 0)),
                  pl.BlockSpec((1, d), lambda j, i: (0, 0)),
                  pl.BlockSpec((tn, d), lambda j, i: (j, 0))],
        out_specs=pl.BlockSpec((tm, tn), lambda j, i: (i, j)),
        out_shape=jax.ShapeDtypeStruct((n, zc), F32),
        compiler_params=_cparams(("arbitrary", "arbitrary")),
        name="inproj",
    )(x, gain, w)


def _swap32(x):
    lane = lax.broadcasted_iota(jnp.int32, x.shape, x.ndim - 1)
    return jnp.where((lane % MLA_ROPE) < MLA_ROPE // 2, pltpu.roll(x, 96, x.ndim - 1), pltpu.roll(x, 32, x.ndim - 1))


def _prep_body(sample, qn_ref, qr_ref, c_ref, krl_ref, cos_ref, sin_ref, lg_ref, qgn_ref, qgr_ref,
               kgn_ref, kgr_ref, wuk_ref, wuv_ref, *outs):
    if sample:
        c_out, qrope_out, qabs_out = outs
    else:
        c_out, q_out, k_out, v_out = outs
    scale = MLA_QK ** -0.5
    cos = cos_ref[...]
    sin = sin_ref[...]
    tm = cos.shape[0]
    lane = lax.broadcasted_iota(jnp.int32, (tm, LANES), 1)
    low_half = lane < MLA_ROPE

    c_raw = c_ref[...]
    c = c_raw * lax.rsqrt(jnp.mean(c_raw * c_raw, axis=-1, keepdims=True) + EPS) * lg_ref[...]
    c_out[...] = c

    qn = qn_ref[...]
    qr = qr_ref[...]
    for hp in range(MLA_HEADS // 2):
        pair = qr[:, hp * LANES:(hp + 1) * LANES]
        pair_sq = pair * pair
        rinv = []
        for h in (2 * hp, 2 * hp + 1):
            qh = qn[:, h * MLA_NOPE:(h + 1) * MLA_NOPE]
            half_sq = jnp.where(low_half if h % 2 == 0 else ~low_half, pair_sq, 0.0)
            ssq = jnp.sum(qh * qh, axis=-1, keepdims=True) + jnp.sum(half_sq, axis=-1, keepdims=True)
            rinv.append(lax.rsqrt(ssq * (1.0 / MLA_QK) + EPS))
        pair_n = pair * jnp.where(low_half, rinv[0], rinv[1]) * qgr_ref[...]
        rot = (pair_n * cos + _swap32(pair_n) * sin) * scale
        if sample:
            qrope_out[:, hp * LANES:(hp + 1) * LANES] = rot.astype(BF16)
        for k, h in enumerate((2 * hp, 2 * hp + 1)):
            qh = qn[:, h * MLA_NOPE:(h + 1) * MLA_NOPE] * rinv[k]
            rot_h = jnp.where(low_half if k == 0 else ~low_half, rot, 0.0)
            if sample:
                qa = (qh * (qgn_ref[...] * kgn_ref[...] * scale)).astype(BF16)
                qabs_out[:, h * KV_RANK:(h + 1) * KV_RANK] = _dot_nt(
                    qa, wuk_ref[:, h * MLA_NOPE:(h + 1) * MLA_NOPE]).astype(BF16)
            else:
                q_out[:, h * 256:h * 256 + LANES] = (qh * (qgn_ref[...] * scale)).astype(BF16)
                q_out[:, h * 256 + LANES:(h + 1) * 256] = rot_h.astype(BF16)

    if not sample:
        cb = c.astype(BF16)
        kn = _dot(cb, wuk_ref[...])
        v_out[...] = _dot(cb, wuv_ref[...]).astype(BF16)
        krl = krl_ref[...]
        kr2 = jnp.where(low_half, krl, pltpu.roll(krl, MLA_ROPE, 1))
        ssq_r = jnp.sum(jnp.where(low_half, krl * krl, 0.0), axis=-1, keepdims=True)
        krg = kr2 * kgr_ref[...]
        krot = krg * cos + pltpu.roll(krg, MLA_ROPE // 2, 1) * sin
        for h in range(MLA_HEADS):
            kh = kn[:, h * MLA_NOPE:(h + 1) * MLA_NOPE]
            ssq = jnp.sum(kh * kh, axis=-1, keepdims=True) + ssq_r
            rinv_k = lax.rsqrt(ssq * (1.0 / MLA_QK) + EPS)
            k_out[:, h * 256:h * 256 + LANES] = (kh * rinv_k * kgn_ref[...]).astype(BF16)
            k_out[:, h * 256 + LANES:(h + 1) * 256] = jnp.where(
                low_half if h % 2 == 0 else ~low_half, krot * rinv_k, 0.0).astype(BF16)


def _prep(z, row0, nrows, tm, cos, sin, sample, lg, qgn, qgr2, kgn, kgr2, wuk, wuv):
    r0 = row0 // tm
    ptiles = cos.shape[0] // tm
    zspec = lambda width, col: pl.BlockSpec((tm, width), lambda i: (r0 + i, col // width))
    const = lambda a: pl.BlockSpec(a.shape, lambda i: (0,) * a.ndim)
    tab = pl.BlockSpec((tm, LANES), lambda i: (i % ptiles, 0))
    in_specs = [zspec(1024, Z_QN), zspec(512, Z_QR), zspec(512, Z_C), zspec(LANES, Z_KRL), tab, tab,
                const(lg), const(qgn), const(qgr2), const(kgn), const(kgr2), const(wuk), const(wuv)]
    row = lambda width: pl.BlockSpec((tm, width), lambda i: (i, 0))
    if sample:
        out_shape = (jax.ShapeDtypeStruct((nrows, KV_RANK), F32),
                     jax.ShapeDtypeStruct((nrows, MLA_HEADS * MLA_ROPE), BF16),
                     jax.ShapeDtypeStruct((nrows, MLA_HEADS * KV_RANK), BF16))
        out_specs = (row(KV_RANK), row(MLA_HEADS * MLA_ROPE), row(MLA_HEADS * KV_RANK))
    else:
        out_shape = (jax.ShapeDtypeStruct((nrows, KV_RANK), F32),
                     jax.ShapeDtypeStruct((nrows, MLA_HEADS * 256), BF16),
                     jax.ShapeDtypeStruct((nrows, MLA_HEADS * 256), BF16),
                     jax.ShapeDtypeStruct((nrows, MLA_HEADS * MLA_V), BF16))
        out_specs = (row(KV_RANK), row(MLA_HEADS * 256), row(MLA_HEADS * 256), row(MLA_HEADS * MLA_V))
    return pl.pallas_call(
        functools.partial(_prep_body, sample),
        grid=(nrows // tm,),
        in_specs=in_specs, out_specs=out_specs, out_shape=out_shape,
        compiler_params=_cparams(("arbitrary",)),
        name="mla_prep_sample" if sample else "mla_prep_prompt",
    )(z, z, z, z, cos, sin, lg, qgn, qgr2, kgn, kgr2, wuk, wuv)


def _pattn_body(q_ref, k_ref, v_ref, o_ref, *, tq):
    sp = q_ref.shape[0]
    row = lax.broadcasted_iota(jnp.int32, (tq, tq), 0)
    col = lax.broadcasted_iota(jnp.int32, (tq, tq), 1)
    for qi in range(sp // tq):
        q = q_ref[qi * tq:(qi + 1) * tq, :]
        s_d = jnp.where(col <= row, _dot_nt(q, k_ref[qi * tq:(qi + 1) * tq, :]), NEG_BIG)
        m = jnp.max(s_d, axis=-1, keepdims=True)
        if qi > 0:
            s_p = _dot_nt(q, k_ref[0:qi * tq, :])
            m = jnp.maximum(m, jnp.max(s_p, axis=-1, keepdims=True))
        p_d = jnp.exp(s_d - m)
        l = jnp.sum(p_d, axis=-1, keepdims=True)
        acc = _dot(p_d.astype(BF16), v_ref[qi * tq:(qi + 1) * tq, :])
        if qi > 0:
            p_p = jnp.exp(s_p - m)
            l = l + jnp.sum(p_p, axis=-1, keepdims=True)
            acc = acc + _dot(p_p.astype(BF16), v_ref[0:qi * tq, :])
        o_ref[qi * tq:(qi + 1) * tq, :] = acc / l


def _prompt_attention(q, k, v, bp, sp, tq):
    return pl.pallas_call(
        functools.partial(_pattn_body, tq=tq),
        grid=(bp, MLA_HEADS),
        in_specs=[pl.BlockSpec((sp, 256), lambda b, h: (b, h)),
                  pl.BlockSpec((sp, 256), lambda b, h: (b, h)),
                  pl.BlockSpec((sp, MLA_V), lambda b, h: (b, h))],
        out_specs=pl.BlockSpec((sp, MLA_V), lambda b, h: (b, h)),
        out_shape=jax.ShapeDtypeStruct((bp * sp, MLA_HEADS * MLA_V), F32),
        compiler_params=_cparams(("arbitrary", "arbitrary")),
        name="prompt_attention",
    )(q, k, v)


SATTN_PPS = 64
SATTN_SUB = 8


def _sample_rope_tables(past, page, rsub):
    half = MLA_ROPE // 2
    inv = (ROPE_BASE ** (-np.arange(half, dtype=np.float64) / half))[:, None]
    ang_b = inv[None] * (np.arange(past // rsub, dtype=np.float64) * rsub)[:, None, None]
    ang_b = np.broadcast_to(ang_b, (past // rsub, half, LANES))
    ang_l = inv * np.arange(rsub, dtype=np.float64)[None, :]
    ang_n = inv * (past + np.arange(page, dtype=np.float64))[None, :]
    f = lambda a: jnp.asarray(a.astype(np.float32))
    return (f(np.cos(ang_b)), f(np.sin(ang_b)), f(np.stack([np.cos(ang_l), np.sin(ang_l)])),
            f(np.cos(ang_n)), f(np.sin(ang_n)))


def _sattn_body(pt_ref, lat_hbm, kr_hbm, cnew_ref, krnew_ref, cosb_ref, sinb_ref, loc_ref, cosn_ref, sinn_ref,
                wuk_ref, qabs_ref, qrope_ref, kgr_ref, o_ref, wext, m_ref, l_ref, acc_ref, latbuf, krbuf,
                lsem, ksem, *, layer, pps, sub, nch, ts, n_pages, page):
    b = pl.program_id(0)
    j = pl.program_id(1)
    nbatch = pl.num_programs(0)
    nrow_q = qabs_ref.shape[0]
    nw = MLA_HEADS * MLA_NOPE
    half_r = MLA_ROPE // 2

    def page_copies(bb, jj, slot):
        out = []
        for i in range(pps):
            pg = pt_ref[bb * n_pages + jj * pps + i]
            out.append(pltpu.make_async_copy(lat_hbm.at[layer, pg], latbuf.at[slot, pl.ds(i * page, page)],
                                             lsem.at[slot]))
            out.append(pltpu.make_async_copy(kr_hbm.at[layer, pg], krbuf.at[slot, :, pl.ds(i * page, page)],
                                             ksem.at[slot]))
        return out

    @pl.when((b == 0) & (j == 0))
    def _():
        wext[0:nw, :] = wuk_ref[...]
        for cp in page_copies(0, 0, 0):
            cp.start()

    @pl.when(j == 0)
    def _():
        wext[nw:nw + nrow_q, :] = qabs_ref[...]
        m_ref[...] = jnp.full(m_ref.shape, NEG_BIG, F32)
        l_ref[...] = jnp.zeros(l_ref.shape, F32)
        acc_ref[...] = jnp.zeros(acc_ref.shape, F32)

    def front(c, kr, cos, sin):
        r = c.shape[0]
        cb = c.astype(BF16)
        nhalf = nw // 2
        big_a = _dot_nt(wext[0:nhalf, :], cb)
        big_b = _dot_nt(wext[nhalf:nw + nrow_q, :], cb)
        krg = kr * jnp.concatenate([kgr_ref[...]] * (r // LANES), axis=1)
        x1, x2 = krg[0:half_r, :], krg[half_r:MLA_ROPE, :]
        krot = jnp.concatenate([x1 * cos - x2 * sin, x2 * cos + x1 * sin], axis=0).astype(BF16)
        s_rope = _dot(qrope_ref[...], krot)
        ssq_r = jnp.sum(kr * kr, axis=0, keepdims=True)
        return cb, big_a, big_b, s_rope, ssq_r

    def back(parts, mask):
        cb, big_a, big_b, s_rope, ssq_r = parts
        r = cb.shape[0]
        nhalf = nw // 2
        hh = MLA_HEADS // 2
        ssq_a = jnp.sum((big_a * big_a).reshape(hh, MLA_NOPE, r), axis=1)
        kn_b = big_b[0:nhalf, :]
        ssq_b = jnp.sum((kn_b * kn_b).reshape(hh, MLA_NOPE, r), axis=1)
        ssq_n = jnp.concatenate([ssq_a, ssq_b], axis=0)
        s_lat = big_b[nhalf:nhalf + nrow_q, :]
        rinv = lax.rsqrt((ssq_n + ssq_r) * (1.0 / MLA_QK) + EPS)
        s = (s_lat + s_rope) * jnp.concatenate([rinv] * ts, axis=0)
        if mask is not None:
            s = jnp.where(mask, s, NEG_BIG)
        m_old = m_ref[...]
        m_new = jnp.maximum(m_old, jnp.max(s, axis=-1, keepdims=True))
        alpha = jnp.exp(m_old - m_new)
        p = jnp.exp(s - m_new)
        l_ref[...] = l_ref[...] * alpha + jnp.sum(p, axis=-1, keepdims=True)
        acc_ref[...] = acc_ref[...] * alpha + _dot(p.astype(BF16), cb)
        m_ref[...] = m_new

    @pl.when(j < nch)
    def _():
        slot = (b * nch + j) % 2
        last_of_seq = j + 1 == nch
        nb = jnp.where(last_of_seq, b + 1, b)
        nj = jnp.where(last_of_seq, 0, j + 1)

        @pl.when(nb < nbatch)
        def _():
            for cp in page_copies(nb, nj, 1 - slot):
                cp.start()

        for cp in page_copies(b, j, slot):
            cp.wait()
        rsub = sub * page

        def chunk_front(g):
            c = latbuf[slot, pl.ds(g * rsub, rsub), :]
            kr = krbuf[slot, :, pl.ds(g * rsub, rsub)]
            base = j * (pps // sub) + g
            cb_ = jnp.concatenate([cosb_ref[base]] * sub, axis=1)
            sb_ = jnp.concatenate([sinb_ref[base]] * sub, axis=1)
            cos = cb_ * loc_ref[0] - sb_ * loc_ref[1]
            sin = sb_ * loc_ref[0] + cb_ * loc_ref[1]
            return front(c, kr, cos, sin)

        ngroups = pps // sub
        parts = chunk_front(0)
        for g in range(ngroups):
            nxt = chunk_front(g + 1) if g + 1 < ngroups else None
            back(parts, None)
            parts = nxt

    @pl.when(j == nch)
    def _():
        r = cnew_ref.shape[0]
        key = lax.broadcasted_iota(jnp.int32, (nrow_q, r), 1)
        tok = lax.broadcasted_iota(jnp.int32, (nrow_q, r), 0) // MLA_HEADS
        back(front(cnew_ref[...], krnew_ref[...], cosn_ref[...], sinn_ref[...]), (key < ts) & (key <= tok))
        o_ref[...] = acc_ref[...] / l_ref[...]


def _sample_attention(layer, cache_latent, cache_k_rope, page_table, cnew, krnew, tabs, wuk_t, qabs, qrope, kgr2,
                      pps, sub):
    bs, n_pages = page_table.shape
    page = cache_latent.shape[2]
    nch = n_pages // pps
    ts = qabs.shape[1] // MLA_HEADS
    cosb, sinb, loc, cosn, sinn = tabs
    pt_flat = page_table.reshape(-1)
    const = lambda a: pl.BlockSpec(a.shape, lambda b, j, pt: (0,) * a.ndim)
    per_seq = lambda a: pl.BlockSpec((None,) + a.shape[1:], lambda b, j, pt: (b,) + (0,) * (a.ndim - 1))
    hbm = pl.BlockSpec(memory_space=pl.ANY)
    in_specs = [hbm, hbm, per_seq(cnew), per_seq(krnew), const(cosb), const(sinb), const(loc),
                const(cosn), const(sinn), const(wuk_t), per_seq(qabs), per_seq(qrope), const(kgr2)]
    nrow_q = qabs.shape[1]
    grid_spec = pltpu.PrefetchScalarGridSpec(
        num_scalar_prefetch=1,
        grid=(bs, nch + 1),
        in_specs=in_specs,
        out_specs=pl.BlockSpec((None, nrow_q, KV_RANK), lambda b, j, pt: (b, 0, 0)),
        scratch_shapes=[pltpu.VMEM((MLA_HEADS * MLA_NOPE + nrow_q, KV_RANK), BF16),
                        pltpu.VMEM((nrow_q, 1), F32), pltpu.VMEM((nrow_q, 1), F32),
                        pltpu.VMEM((nrow_q, KV_RANK), F32),
                        pltpu.VMEM((2, pps * page, KV_RANK), F32),
                        pltpu.VMEM((2, MLA_ROPE, pps * page), F32),
                        pltpu.SemaphoreType.DMA((2,)), pltpu.SemaphoreType.DMA((2,))])
    return pl.pallas_call(
        functools.partial(_sattn_body, layer=layer, pps=pps, sub=sub, nch=nch, ts=ts, n_pages=n_pages, page=page),
        grid_spec=grid_spec,
        out_shape=jax.ShapeDtypeStruct((bs, nrow_q, KV_RANK), F32),
        compiler_params=_cparams(("arbitrary", "arbitrary")),
        name="sample_attention",
    )(pt_flat, cache_latent, cache_k_rope, cnew, krnew, cosb, sinb, loc, cosn, sinn, wuk_t, qabs, qrope, kgr2)


def _uvproj_body(o_ref, w_ref, out_ref):
    out_ref[...] = _dot(o_ref[...].astype(BF16), w_ref[...])


def _uv_project(olat, wuv):
    ns = olat.shape[0]
    return pl.pallas_call(
        _uvproj_body,
        grid=(MLA_HEADS,),
        in_specs=[pl.BlockSpec((ns, KV_RANK), lambda h: (0, h)),
                  pl.BlockSpec((KV_RANK, MLA_V), lambda h: (0, h))],
        out_specs=pl.BlockSpec((ns, MLA_V), lambda h: (0, h)),
        out_shape=jax.ShapeDtypeStruct((ns, MLA_HEADS * MLA_V), F32),
        compiler_params=_cparams(("arbitrary",)),
        name="uv_project",
    )(olat, wuv)


CONV_PAD = 32


def _conv_body(a_ref, g_ref, st_ref, w_ref, b_ref, lng_ref, lnb_ref, o_ref, st_out_ref, ext, *, nseq, t, rc):
    ctx = CONV_WIDTH - 1
    u = a_ref[...] * jax.nn.sigmoid(g_ref[...])
    for s in range(nseq):
        ext[s, 0:SUBLANES, :] = jnp.zeros((SUBLANES, CONV_CH), F32)
        ext[s, CONV_PAD - ctx:CONV_PAD, :] = st_ref[s]
        ext[s, CONV_PAD:CONV_PAD + t, :] = u[s * t:(s + 1) * t, :]
        ext[s, CONV_PAD + t:, :] = jnp.zeros((ext.shape[1] - CONV_PAD - t, CONV_CH), F32)
    bias = b_ref[...]
    lng = lng_ref[...]
    lnb = lnb_ref[...]

    def chunk(s, r0):
        span = ((rc + SUBLANES - 1) // SUBLANES) * SUBLANES
        win = ext[s, pl.ds(r0, span + CONV_PAD + SUBLANES), :]
        acc = jnp.zeros((rc, CONV_CH), F32) + bias
        for b in range(SUBLANES):
            off = CONV_PAD - ctx + b
            sh = win[off:off + span + CONV_PAD - SUBLANES, :]
            for a in range((CONV_WIDTH - b + SUBLANES - 1) // SUBLANES):
                w = SUBLANES * a + b
                acc = acc + sh[SUBLANES * a:SUBLANES * a + rc, :] * w_ref[w:w + 1, :]
        mu = jnp.mean(acc, axis=-1, keepdims=True)
        cen = acc - mu
        var = jnp.mean(cen * cen, axis=-1, keepdims=True)
        y = cen * lax.rsqrt(var + EPS) * lng + lnb
        o_ref[pl.ds(s * t + r0, rc), :] = y * jax.nn.sigmoid(y)

    for s in range(nseq):
        if t // rc == 1:
            chunk(s, 0)
        else:
            def body(i, carry, s=s):
                chunk(s, pl.multiple_of(i * rc, rc))
                return carry
            lax.fori_loop(0, t // rc, body, 0)
        st_out_ref[s] = ext[s, t + CONV_PAD - ctx:t + CONV_PAD, :]


def _conv(z, row0, state, nseq, t, cw, cb, lng, lnb):
    bsz = state.shape[0]
    rows = nseq * t
    r0 = row0 // rows
    rc = min(t, 32)
    const = lambda a: pl.BlockSpec(a.shape, lambda i: (0,) * a.ndim)
    return pl.pallas_call(
        functools.partial(_conv_body, nseq=nseq, t=t, rc=rc),
        grid=(bsz // nseq,),
        in_specs=[pl.BlockSpec((rows, CONV_CH), lambda i: (r0 + i, Z_CA // CONV_CH)),
                  pl.BlockSpec((rows, CONV_CH), lambda i: (r0 + i, Z_CG // CONV_CH)),
                  pl.BlockSpec((nseq, CONV_WIDTH - 1, CONV_CH), lambda i: (i, 0, 0)),
                  const(cw), const(cb), const(lng), const(lnb)],
        out_specs=(pl.BlockSpec((rows, CONV_CH), lambda i: (i, 0)),
                   pl.BlockSpec((nseq, CONV_WIDTH - 1, CONV_CH), lambda i: (i, 0, 0))),
        out_shape=(jax.ShapeDtypeStruct((bsz * t, CONV_CH), F32),
                   jax.ShapeDtypeStruct((bsz, CONV_WIDTH - 1, CONV_CH), F32)),
        scratch_shapes=[pltpu.VMEM((nseq, CONV_PAD + max(t, SUBLANES) + SUBLANES, CONV_CH), F32)],
        compiler_params=_cparams(("arbitrary",)),
        name="conv_module_t%d" % t,
    )(z, z, state, cw, cb, lng, lnb)


def _gla_matrices(rows, seq_len):
    t = np.arange(rows)[:, None]
    u = np.arange(rows)[None, :]
    same = (t // seq_len) == (u // seq_len)
    blocks = [same & (u <= t), same & (u > t)]
    nlev = int(round(math.log2(seq_len)))
    for j in range(nlev):
        h = 2 ** j
        mid = (t // (2 * h)) * (2 * h) + h - 1
        second = (t % (2 * h)) >= h
        blocks.append(np.where(second, (u > mid) & (u <= t), (u > t) & (u <= mid)))
    return np.concatenate(blocks, axis=0).astype(np.float32), nlev


def _gla_body(gq_ref, gk_ref, gv_ref, krl_ref, go_ref, s0_ref, gwp_ref, gb_ref, ng_ref, mall_ref,
              o_ref, sout_ref, st, qd_s, kd_s, ghi_s, glo_s, oi_s, *, seq_len, nlev):
    ci = pl.program_id(1)
    rows = gq_ref.shape[0]
    nh, dk, dv = GLA_HEADS, GLA_DK, GLA_DV
    rg = max(seq_len, SUBLANES)
    spg = rg // seq_len

    @pl.when(ci == 0)
    def _():
        st[...] = s0_ref[...]

    a_hi, a_lo = _split_hi_lo(krl_ref[...])
    w_hi, w_lo = _split_hi_lo(gwp_ref[...])
    x = _dot(a_hi, w_hi) + _dot(a_lo, w_hi) + _dot(a_hi, w_lo) + gb_ref[...]
    g = jax.nn.log_sigmoid(x) * (1.0 / GLA_TAU)
    g_hi, g_lo = _split_hi_lo(g)
    mall = mall_ref[...]
    d = _dot(mall, g_hi) + _dot(mall, g_lo)
    q = gq_ref[...] * (dk ** -0.5)
    k = gk_ref[...]
    v = gv_ref[...].astype(BF16)
    lane_head = lax.broadcasted_iota(jnp.int32, (rows, nh * dk), 1) // dk
    tcol = lax.broadcasted_iota(jnp.int32, (rows, 1), 0)
    rowi = lax.broadcasted_iota(jnp.int32, (rows, rows), 0)
    coli = lax.broadcasted_iota(jnp.int32, (rows, rows), 1)

    def head_stack(a):
        return jnp.concatenate([jnp.where(lane_head == h, a, 0.0) for h in range(nh)], axis=0).astype(BF16)

    p = _dot_nt(head_stack(q), k.astype(BF16))
    amat = [jnp.where(rowi == coli, p[h * rows:(h + 1) * rows], 0.0) for h in range(nh)]
    for j in range(nlev):
        half = 2 ** j
        e = jnp.exp(d[(2 + j) * rows:(3 + j) * rows])
        second = (tcol % (2 * half)) >= half
        qj = jnp.where(second, q * e, 0.0)
        kj = jnp.where(second, 0.0, k * e)
        p = _dot_nt(head_stack(qj), kj.astype(BF16))
        same = (rowi // (2 * half)) == (coli // (2 * half))
        amat = [amat[h] + jnp.where(same, p[h * rows:(h + 1) * rows], 0.0) for h in range(nh)]
    for h in range(nh):
        oi_s[:, h * dv:(h + 1) * dv] = _dot(amat[h].astype(BF16), v[:, h * dv:(h + 1) * dv])

    qd_s[...] = q * jnp.exp(d[0:rows])
    kd_s[...] = k * jnp.exp(d[rows:2 * rows])
    ghi_s[...] = g_hi.astype(F32)
    glo_s[...] = g_lo.astype(F32)

    ones = jnp.ones((rg, dv), BF16)
    gl_head = lax.broadcasted_iota(jnp.int32, (rg, nh * dk), 1) // dk
    g_row = lax.broadcasted_iota(jnp.int32, (rg, 1), 0) // seq_len
    st_head = lax.broadcasted_iota(jnp.int32, (nh * dk, dv), 0) // dk

    def group(gi, carry):
        r0 = pl.multiple_of(gi * rg, rg)
        qd = qd_s[pl.ds(r0, rg), :]
        kd = kd_s[pl.ds(r0, rg), :]
        ghi = ghi_s[pl.ds(r0, rg), :].astype(BF16)
        glo = glo_s[pl.ds(r0, rg), :].astype(BF16)
        vg = gv_ref[pl.ds(r0, rg), :].astype(BF16)
        o_inter = [jnp.zeros((rg, dv), F32) for _ in range(nh)]
        for w in range(spg):
            sidx = gi * spg + w
            s_old = st[sidx]
            s_bf = s_old.astype(BF16)
            mine = g_row == w
            qw = jnp.where(mine, qd, 0.0)
            kw = jnp.where(mine, kd, 0.0).astype(BF16)
            bl = (_dot_tn(jnp.where(mine, ghi, jnp.zeros_like(ghi)), ones)
                  + _dot_tn(jnp.where(mine, glo, jnp.zeros_like(glo)), ones))
            upd = jnp.zeros((nh * dk, dv), F32)
            for h in range(nh):
                o_inter[h] = o_inter[h] + _dot(jnp.where(gl_head == h, qw, 0.0).astype(BF16), s_bf)
                u_h = _dot_tn(kw, vg[:, h * dv:(h + 1) * dv])
                upd = upd + jnp.where(st_head == h, u_h, 0.0)
            st[sidx] = jnp.exp(bl) * s_old + upd
        for h in range(nh):
            oi_s[pl.ds(r0, rg), h * dv:(h + 1) * dv] = oi_s[pl.ds(r0, rg), h * dv:(h + 1) * dv] + o_inter[h]
        return carry

    ngroups = rows // rg
    if ngroups == 1:
        group(0, 0)
    else:
        lax.fori_loop(0, ngroups, group, 0)

    go = go_ref[...]
    for h in range(nh):
        oh = oi_s[:, h * dv:(h + 1) * dv]
        on = oh * lax.rsqrt(jnp.mean(oh * oh, axis=-1, keepdims=True) + EPS) * ng_ref[...]
        gh = go[:, h * dv:(h + 1) * dv]
        o_ref[:, h * dv:(h + 1) * dv] = on * (gh * jax.nn.sigmoid(gh))

    @pl.when(ci == pl.num_programs(1) - 1)
    def _():
        sout_ref[...] = st[...]


def _gla(z, row0, s0, rows, seq_len, nchunks, gwp, gb, ng):
    bsz = s0.shape[0]
    nseq = rows // seq_len if nchunks == 1 else 1
    ngrid = bsz // nseq
    mall_np, nlev = _gla_matrices(rows, seq_len)
    mall = jnp.asarray(mall_np, BF16)
    r0 = row0 // rows
    nk = GLA_HEADS * GLA_DK
    nv = GLA_HEADS * GLA_DV
    zspec = lambda width, col: pl.BlockSpec((rows, width), lambda b, c: (r0 + b * nchunks + c, col // width))
    const = lambda a: pl.BlockSpec(a.shape, lambda b, c: (0,) * a.ndim)
    sspec = pl.BlockSpec((nseq, nk, GLA_DV), lambda b, c: (b, 0, 0))
    return pl.pallas_call(
        functools.partial(_gla_body, seq_len=seq_len, nlev=nlev),
        grid=(ngrid, nchunks),
        in_specs=[zspec(nk, Z_GQ), zspec(nk, Z_GK), zspec(nv, Z_GV), zspec(LANES, Z_KRL), zspec(nv, Z_GO),
                  sspec, const(gwp), const(gb), const(ng), const(mall)],
        out_specs=(pl.BlockSpec((rows, nv), lambda b, c: (b * nchunks + c, 0)), sspec),
        out_shape=(jax.ShapeDtypeStruct((ngrid * nchunks * rows, nv), F32),
                   jax.ShapeDtypeStruct((bsz, nk, GLA_DV), F32)),
        scratch_shapes=[pltpu.VMEM((nseq, nk, GLA_DV), F32),
                        pltpu.VMEM((rows, nk), F32), pltpu.VMEM((rows, nk), F32),
                        pltpu.VMEM((rows, nk), F32), pltpu.VMEM((rows, nk), F32),
                        pltpu.VMEM((rows, nv), F32)],
        compiler_params=_cparams(("arbitrary", "arbitrary")),
        name="gla_l%d" % seq_len,
    )(z, z, z, z, z, s0, gwp, gb, ng, mall)


def _outproj_body(x_ref, mp_ref, cp_ref, gp_ref, ms_ref, cs_ref, gs_ref, w_ref, o_ref, *, np_tiles):
    is_sample = pl.program_id(1) >= np_tiles
    pick = lambda p_ref, s_ref: jnp.where(is_sample, s_ref[...], p_ref[...])
    mix = jnp.concatenate([pick(mp_ref, ms_ref), pick(cp_ref, cs_ref), pick(gp_ref, gs_ref)], axis=1).astype(BF16)
    o_ref[...] = x_ref[...] + _dot(mix, w_ref[...])


def _outproj(x, mixers_p, mixers_s, w, tm):
    n, d = x.shape
    tn = d // 2
    np_tiles = mixers_p[0].shape[0] // tm
    pspec = lambda a: pl.BlockSpec((tm, a.shape[1]), lambda j, i: (jnp.minimum(i, np_tiles - 1), 0))
    sspec = lambda a: pl.BlockSpec((tm, a.shape[1]), lambda j, i: (jnp.maximum(i - np_tiles, 0), 0))
    return pl.pallas_call(
        functools.partial(_outproj_body, np_tiles=np_tiles),
        grid=(2, n // tm),
        in_specs=([pl.BlockSpec((tm, tn), lambda j, i: (i, j))]
                  + [pspec(a) for a in mixers_p] + [sspec(a) for a in mixers_s]
                  + [pl.BlockSpec((w.shape[0], tn), lambda j, i: (0, j))]),
        out_specs=pl.BlockSpec((tm, tn), lambda j, i: (i, j)),
        out_shape=jax.ShapeDtypeStruct((n, d), F32),
        compiler_params=_cparams(("arbitrary", "arbitrary")),
        name="outproj",
    )(x, *mixers_p, *mixers_s, w)


def _router_body(x_ref, g_ref, w_ref, b_ref, h_ref, e_ref, gate_ref):
    x = x_ref[...]
    h = (x * lax.rsqrt(jnp.mean(x * x, axis=-1, keepdims=True) + EPS)) * g_ref[...]
    h_ref[...] = h
    h_hi, h_lo = _split_hi_lo(h)
    w_hi, w_lo = _split_hi_lo(w_ref[...])
    logits = _dot(h_hi, w_hi) + _dot(h_lo, w_hi) + _dot(h_hi, w_lo) + b_ref[...]
    lane = lax.broadcasted_iota(jnp.int32, logits.shape, 1)
    far = jnp.int32(4 * LANES)
    red_max = lambda a: jnp.max(a, axis=-1, keepdims=True)
    red_min = lambda a: jnp.min(a, axis=-1, keepdims=True)
    red_sum = lambda a: jnp.sum(a, axis=-1, keepdims=True)

    is_g = lane < N_GROUPS
    eg = jnp.where(is_g, jnp.exp(logits - red_max(jnp.where(is_g, logits, NEG_BIG))), 0.0)
    pg = eg / red_sum(eg)
    g_prob = red_max(pg)
    g_idx = red_min(jnp.where(is_g & (pg == g_prob), lane, far))
    lo = N_GROUPS + g_idx * EXPERTS_PER_GROUP
    in_g = (lane >= lo) & (lane < lo + EXPERTS_PER_GROUP)
    ee = jnp.where(in_g, jnp.exp(logits - red_max(jnp.where(in_g, logits, NEG_BIG))), 0.0)
    pe = ee / red_sum(ee)
    p1 = red_max(pe)
    i1 = red_min(jnp.where(in_g & (pe == p1), lane, far))
    rest = in_g & (lane != i1)
    p2 = red_max(jnp.where(rest, pe, -1.0))
    i2 = red_min(jnp.where(rest & (pe == p2), lane, far))
    den = p1 + p2
    e_ref[...] = jnp.where(lane == 0, i1 - N_GROUPS, jnp.where(lane == 1, i2 - N_GROUPS, 0))
    gate_ref[...] = jnp.where(lane == 0, g_prob * p1 / den, jnp.where(lane == 1, g_prob * p2 / den, 0.0))


def _router(x, gain, w, b, tm):
    n, d = x.shape
    const = lambda a: pl.BlockSpec(a.shape, lambda i: (0,) * a.ndim)
    return pl.pallas_call(
        _router_body,
        grid=(n // tm,),
        in_specs=[pl.BlockSpec((tm, d), lambda i: (i, 0)), const(gain), const(w), const(b)],
        out_specs=(pl.BlockSpec((tm, d), lambda i: (i, 0)),
                   pl.BlockSpec((tm, LANES), lambda i: (i, 0)),
                   pl.BlockSpec((tm, LANES), lambda i: (i, 0))),
        out_shape=(jax.ShapeDtypeStruct((n, d), F32),
                   jax.ShapeDtypeStruct((n, LANES), jnp.int32),
                   jax.ShapeDtypeStruct((n, LANES), F32)),
        compiler_params=_cparams(("arbitrary",)),
        name="moe_router",
    )(x, gain, w, b)


def _ffn_body(bexp_ref, rtok_ref, rout_ref, h_hbm, gate_ref, wg_ref, wu_ref, wd_ref, oa_hbm,
              xbuf, obuf, wgb, wub, wdb, gsem, ssem, *, bm):
    i = pl.program_id(0)
    nb = pl.num_programs(0)
    slot = i % 2
    nxt = jnp.minimum(i + 1, nb - 1)
    prev = jnp.where(i > 0, i - 1, nb)

    def gather_start(step, sl):
        for r in range(bm):
            pltpu.make_async_copy(h_hbm.at[rtok_ref[step * bm + r]], xbuf.at[sl, r], gsem.at[sl]).start(priority=r % 2)

    def gather_wait(sl):
        for r in range(bm):
            pltpu.make_async_copy(h_hbm.at[0], xbuf.at[sl, r], gsem.at[sl]).wait()

    def scatter_start(step, sl):
        for r in range(bm):
            pltpu.make_async_copy(obuf.at[sl, r], oa_hbm.at[rout_ref[step * bm + r]], ssem).start(priority=r % 2)

    def scatter_wait():
        for r in range(bm):
            pltpu.make_async_copy(obuf.at[0, r], oa_hbm.at[0], ssem).wait()

    @pl.when(i == 0)
    def _():
        gather_start(0, 0)
        obuf[...] = jnp.zeros(obuf.shape, F32)
        scatter_start(nb + 1, 0)

    last = bexp_ref[jnp.maximum(i - 1, 0)]
    @pl.when((i == 0) | (bexp_ref[i] != last))
    def _():
        wgb[...] = wg_ref[...].astype(BF16)
        wub[...] = wu_ref[...].astype(BF16)
        wdb[...] = wd_ref[...].astype(BF16)

    gather_wait(slot)
    scatter_wait()
    x = xbuf[slot].astype(BF16)
    gather_start(nxt, 1 - slot)
    scatter_start(prev, 1 - slot)
    a = _dot_nt(x, wgb[...])
    u = _dot_nt(x, wub[...])
    mid = (a * jax.nn.sigmoid(a) * u).astype(BF16)
    obuf[slot] = _dot(mid, wdb[...]) * gate_ref[...]

    @pl.when(i == nb - 1)
    def _():
        scatter_start(i, slot)
        gather_wait(1 - slot)
        scatter_wait()
        scatter_wait()


def _expert_ffn(layer, h, block_exp, row_tok, row_out, row_gate, wg_t, wu_t, w_down, bm):
    n, d = h.shape
    nb = block_exp.shape[0]
    de = w_down.shape[2]
    wspec = pl.BlockSpec((None, None, de, d), lambda i, be, rt, ro: (layer, be[i], 0, 0))
    grid_spec = pltpu.PrefetchScalarGridSpec(
        num_scalar_prefetch=3,
        grid=(nb,),
        in_specs=[pl.BlockSpec(memory_space=pl.ANY),
                  pl.BlockSpec((bm, 1), lambda i, be, rt, ro: (i, 0)),
                  wspec, wspec, wspec],
        out_specs=pl.BlockSpec(memory_space=pl.ANY),
        scratch_shapes=[pltpu.VMEM((2, bm, d), F32), pltpu.VMEM((2, bm, d), F32),
                        pltpu.VMEM((de, d), BF16), pltpu.VMEM((de, d), BF16), pltpu.VMEM((de, d), BF16),
                        pltpu.SemaphoreType.DMA((2,)), pltpu.SemaphoreType.DMA(())])
    return pl.pallas_call(
        functools.partial(_ffn_body, bm=bm),
        grid_spec=grid_spec,
        out_shape=jax.ShapeDtypeStruct(((nb + 2) * bm, d), F32),
        compiler_params=pltpu.CompilerParams(dimension_semantics=("arbitrary",),
                                             vmem_limit_bytes=FFN_VMEM_LIMIT_BYTES),
        name="expert_ffn",
    )(block_exp, row_tok, row_out, h, row_gate, wg_t, wu_t, w_down)


def _combine_body(x_ref, a_ref, b_ref, o_ref):
    o_ref[...] = x_ref[...] + (a_ref[...] + b_ref[...])


def _combine(x, oa, tm):
    n, d = x.shape
    nt = n // tm
    return pl.pallas_call(
        _combine_body,
        grid=(nt,),
        in_specs=[pl.BlockSpec((tm, d), lambda i: (i, 0)), pl.BlockSpec((tm, d), lambda i: (i, 0)),
                  pl.BlockSpec((tm, d), lambda i: (nt + i, 0))],
        out_specs=pl.BlockSpec((tm, d), lambda i: (i, 0)),
        out_shape=jax.ShapeDtypeStruct((n, d), F32),
        compiler_params=_cparams(("arbitrary",)),
        name="moe_combine",
    )(x, oa, oa)


def _dispatch_plan(experts, gates, bm):
    n = experts.shape[0]
    a = n * TOP_K
    e_flat = experts.reshape(a)
    g_flat = gates.reshape(a)
    onehot = (e_flat[:, None] == jnp.arange(N_EXPERTS, dtype=jnp.int32)[None, :]).astype(jnp.int32)
    csum = jnp.cumsum(onehot, axis=0)
    counts = csum[-1]
    rank = jnp.take_along_axis(csum, e_flat[:, None], axis=1)[:, 0] - 1
    padded = (counts + bm - 1) // bm * bm
    pad_end = jnp.cumsum(padded)
    pad_start = pad_end - padded
    dest = pad_start[e_flat] + rank
    nb = -(-(a + N_EXPERTS * (bm - 1)) // bm)
    rows = nb * bm
    row_asg = jnp.full((rows,), -1, jnp.int32).at[dest].set(jnp.arange(a, dtype=jnp.int32))
    valid = row_asg >= 0
    safe = jnp.maximum(row_asg, 0)
    row_tok = jnp.where(valid, safe // TOP_K, 0).astype(jnp.int32)
    row_gate = jnp.where(valid, g_flat[safe], 0.0).astype(F32)[:, None]
    spare = a + jnp.cumsum(jnp.logical_not(valid).astype(jnp.int32)) - 1
    row_out = jnp.where(valid, (safe % TOP_K) * n + safe // TOP_K, spare).astype(jnp.int32)
    row_out = jnp.concatenate([row_out, rows + jnp.arange(2 * bm, dtype=jnp.int32)])
    starts = jnp.arange(nb, dtype=jnp.int32) * bm
    block_exp = jnp.minimum(jnp.sum((pad_end[None, :] <= starts[:, None]).astype(jnp.int32), axis=1),
                            N_EXPERTS - 1).astype(jnp.int32)
    return block_exp, row_tok, row_out, row_gate


FFN_BLOCK_ROWS = 256


def kernel(x_prompt, x_sample, cache_latent, cache_k_rope, state_conv, state_gla, page_table, ln_mix, w_in,
           latent_gain, q_gain, k_gain, w_uk, w_uv, conv_w, conv_b, conv_ln_g, conv_ln_b, gla_gate_w, gla_gate_b,
           gla_norm_g, w_out, ln_ffn, router_group_w, router_group_b, router_expert_w, router_expert_b,
           w_gate, w_up, w_down):
    bp, sp, d = x_prompt.shape
    bs, ts, _ = x_sample.shape
    depth = w_in.shape[0]
    n_p, n_s = bp * sp, bs * ts
    n = n_p + n_s
    n_pages = page_table.shape[1]
    page = cache_latent.shape[2]
    past = n_pages * page
    tm = math.gcd(math.gcd(n_p, n_s), 512)
    tmb = math.gcd(tm, 256)
    tq = math.gcd(sp, 256)
    pps = min(SATTN_PPS, n_pages)
    sub = min(SATTN_SUB, pps)
    assert n_pages % pps == 0 and pps % sub == 0 and sp % tmb == 0 and tmb % ts == 0
    gla_rows = min(128, sp)
    gla_rows_s = min(128, n_s)
    assert sp % gla_rows == 0 and n_s % gla_rows_s == 0 and gla_rows_s % max(ts, SUBLANES) == 0
    assert n_p % gla_rows_s == 0 and (ts & (ts - 1)) == 0
    conv_nseq = SUBLANES
    assert bs % conv_nseq == 0 and n_p % (conv_nseq * ts) == 0

    cos_p, sin_p = _rope_tables(np.arange(sp))
    cos_s, sin_s = _rope_tables(past + (np.arange(tmb) % ts))
    tabs = _sample_rope_tables(past, page, sub * page)
    cos_p, sin_p, cos_s, sin_s = (jnp.asarray(t) for t in (cos_p, sin_p, cos_s, sin_s))
    kr_cache_t = jnp.swapaxes(cache_k_rope, 2, 3)
    wg_t = jnp.swapaxes(w_gate, 2, 3)
    wu_t = jnp.swapaxes(w_up, 2, 3)
    w_in_t = jnp.swapaxes(w_in, 1, 2)

    src = _z_source_columns()
    src_idx = jnp.asarray(np.maximum(src, 0), jnp.int32)
    src_ok = jnp.asarray(src >= 0)

    x = jnp.concatenate([x_prompt.reshape(n_p, d), x_sample.reshape(n_s, d)], axis=0)
    zeros_conv = jnp.zeros((bp, CONV_WIDTH - 1, CONV_CH), F32)
    zeros_gla = jnp.zeros((bp, GLA_HEADS * GLA_DK, GLA_DV), F32)
    outs = [[] for _ in range(8)]

    for l in range(depth):
        row2 = lambda a: a.reshape(1, -1).astype(F32)
        w_in_p = jnp.where(src_ok[:, None], jnp.take(w_in_t[l], src_idx, axis=0), 0.0).astype(BF16)
        wuk = w_uk[l].reshape(KV_RANK, MLA_HEADS * MLA_NOPE).astype(BF16)
        wuv = w_uv[l].reshape(KV_RANK, MLA_HEADS * MLA_V).astype(BF16)
        qgn, kgn = row2(q_gain[l, :MLA_NOPE]), row2(k_gain[l, :MLA_NOPE])
        qgr2 = row2(jnp.tile(q_gain[l, MLA_NOPE:], 2))
        kgr2 = row2(jnp.tile(k_gain[l, MLA_NOPE:], 2))
        lg = row2(latent_gain[l])

        z = _inproj(x, row2(ln_mix[l]), w_in_p, tm)

        c_p, q_p, k_p, v_p = _prep(z, 0, n_p, tmb, cos_p, sin_p, False, lg, qgn, qgr2, kgn, kgr2, wuk, wuv)
        c_s, qrope_s, qabs_s = _prep(z, n_p, n_s, tmb, cos_s, sin_s, True, lg, qgn, qgr2, kgn, kgr2, wuk, wuv)
        mla_p = _prompt_attention(q_p, k_p, v_p, bp, sp, tq)
        kr_all = z[:, Z_KRL:Z_KRL + MLA_ROPE]
        cnew = jnp.pad(c_s.reshape(bs, ts, KV_RANK), ((0, 0), (0, page - ts), (0, 0)))
        krnew = jnp.pad(jnp.swapaxes(kr_all[n_p:].reshape(bs, ts, MLA_ROPE), 1, 2),
                        ((0, 0), (0, 0), (0, page - ts)))
        kg_lanes = jnp.broadcast_to(k_gain[l, MLA_NOPE:].astype(F32)[:, None], (MLA_ROPE, LANES))
        o_lat = _sample_attention(l, cache_latent, kr_cache_t, page_table, cnew, krnew, tabs, wuk.T,
                                  qabs_s.reshape(bs, ts * MLA_HEADS, KV_RANK),
                                  qrope_s.reshape(bs, ts * MLA_HEADS, MLA_ROPE), kg_lanes, pps, sub)
        mla_s = _uv_project(o_lat.reshape(n_s, MLA_HEADS * KV_RANK), wuv)

        cw, cb = conv_w[l].astype(F32), row2(conv_b[l])
        clg, clb = row2(conv_ln_g[l]), row2(conv_ln_b[l])
        conv_p, cst_p = _conv(z, 0, zeros_conv, 1, sp, cw, cb, clg, clb)
        conv_s, cst_s = _conv(z, n_p, state_conv[l], conv_nseq, ts, cw, cb, clg, clb)

        gwp = jnp.zeros((LANES, GLA_HEADS * GLA_DK), F32).at[MLA_ROPE:MLA_ROPE + GLA_GATE_RANK].set(gla_gate_w[l])
        ggb, gng = row2(gla_gate_b[l]), row2(gla_norm_g[l])
        gla_p, gst_p = _gla(z, 0, zeros_gla, gla_rows, gla_rows, sp // gla_rows, gwp, ggb, gng)
        gla_s, gst_s = _gla(z, n_p, state_gla[l].reshape(bs, GLA_HEADS * GLA_DK, GLA_DV), gla_rows_s, ts, 1,
                            gwp, ggb, gng)

        x = _outproj(x, (mla_p, conv_p, gla_p), (mla_s, conv_s, gla_s), w_out[l].astype(BF16), tm)

        wr = jnp.zeros((d, LANES), F32)
        wr = wr.at[:, :N_GROUPS].set(router_group_w[l]).at[:, N_GROUPS:N_GROUPS + N_EXPERTS].set(router_expert_w[l])
        br = jnp.zeros((1, LANES), F32)
        br = br.at[0, :N_GROUPS].set(router_group_b[l]).at[0, N_GROUPS:N_GROUPS + N_EXPERTS].set(router_expert_b[l])
        h, e_idx, gate = _router(x, row2(ln_ffn[l]), wr, br, tm)
        plan = _dispatch_plan(e_idx[:, :TOP_K], gate[:, :TOP_K], FFN_BLOCK_ROWS)
        oa = _expert_ffn(l, h, *plan, wg_t, wu_t, w_down, FFN_BLOCK_ROWS)
        x = _combine(x, oa, tm)

        outs[0].append(c_p.reshape(bp, sp, KV_RANK))
        outs[1].append(kr_all[:n_p].reshape(bp, sp, MLA_ROPE))
        outs[2].append(cst_p)
        outs[3].append(gst_p.reshape(bp, GLA_HEADS, GLA_DK, GLA_DV))
        outs[4].append(c_s.reshape(bs, ts, KV_RANK))
        outs[5].append(kr_all[n_p:].reshape(bs, ts, MLA_ROPE))
        outs[6].append(cst_s)
        outs[7].append(gst_s.reshape(bs, GLA_HEADS, GLA_DK, GLA_DV))

    return (x[:n_p].reshape(bp, sp, d), x[n_p:].reshape(bs, ts, d)) + tuple(jnp.stack(o) for o in outs)
```
